```python
import math
import jax, jax.numpy as jnp
from jax import lax
import numpy as np

D_MODEL = 2048
BATCH = 4
SEQ = 4096
DEPTH = 4

N_HEADS = 16
HEAD_DIM = D_MODEL // N_HEADS
ATT_WIDTH = N_HEADS * HEAD_DIM
N_A = DEPTH // 2
N_B = DEPTH - N_A
MOBA_BLOCK = 256
MOBA_TOPK = 3
FOX_QBLOCK = 128
N_EXPERTS = 16
N_GROUPS = 4
EXPERTS_PER_GROUP = N_EXPERTS // N_GROUPS
TOP_K = 2
D_FF_EXPERT = 1408
MOE_ROW_BLOCK = 256
N_MOD = 6
FORGET_BIAS_INIT = 5.0
MOD_SCALE = 0.5
EPS = 1e-6

kernel_name = "yoco_moba_fox_grouped_moe_adaln"


def rms_norm(x, g):
    xf = x.astype(jnp.float32)
    y = xf * lax.rsqrt(jnp.mean(xf * xf, axis=-1, keepdims=True) + EPS)
    return (y * g.astype(jnp.float32)).astype(x.dtype)


def modulate(h, shift, scale):
    return h * (1.0 + scale[:, None, :]) + shift[:, None, :]


def alibi_slopes(n_heads):
    return jnp.exp2(-8.0 * jnp.arange(1, n_heads + 1, dtype=jnp.float32) / n_heads)


def moba_attention(q, k, v, slopes):
    B, H, S, hd = q.shape
    MB = MOBA_BLOCK
    nb = -(-S // MB)
    padw = ((0, 0), (0, 0), (0, nb * MB - S), (0, 0))
    qf = jnp.pad(q, padw).reshape(B * H, nb, MB, hd)
    kf = jnp.pad(k, padw).reshape(B * H, nb, MB, hd)
    vf = jnp.pad(v, padw).reshape(B * H, nb, MB, hd)
    k_mean = jnp.mean(kf.astype(jnp.float32), axis=2)
    slope_f = jnp.broadcast_to(slopes[None, :], (B, H)).reshape(B * H)
    scale = hd ** -0.5
    pos = jnp.arange(MB)
    causal = pos[:, None] >= pos[None, :]
    own_dist = (pos[:, None] - pos[None, :]).astype(jnp.float32)
    outs = []
    for i in range(nb):
        q_i, k_i, v_i = qf[:, i], kf[:, i], vf[:, i]
        own = jnp.einsum('nqd,nkd->nqk', q_i, k_i).astype(jnp.float32) * scale
        own = own - slope_f[:, None, None] * own_dist
        own = jnp.where(causal[None], own, -jnp.inf)
        if i == 0:
            p = jax.nn.softmax(own, axis=-1).astype(v_i.dtype)
            outs.append(jnp.einsum('nqk,nkd->nqd', p, v_i))
            continue
        n_sel = min(MOBA_TOPK, i)
        gate = jnp.einsum('nqd,nbd->nqb', q_i.astype(jnp.float32), k_mean[:, :i])
        _, sel = lax.top_k(gate, n_sel)
        q_pos = i * MB + pos

        def attend(args, n_sel=n_sel, q_pos=q_pos):
            q_h, kb_h, vb_h, sel_h, own_h, v_own, m_h = args
            gk = kb_h[sel_h]
            gv = vb_h[sel_h]
            lg = jnp.einsum('qd,qcsd->qcs', q_h, gk).astype(jnp.float32) * scale
            k_pos = sel_h[:, :, None] * MB + pos[None, None, :]
            lg = lg - m_h * (q_pos[:, None, None] - k_pos).astype(jnp.float32)
            lg_all = jnp.concatenate([lg.reshape(MB, n_sel * MB), own_h], axis=-1)
            p = jax.nn.softmax(lg_all, axis=-1).astype(v_own.dtype)
            p_sel = p[:, :n_sel * MB].reshape(MB, n_sel, MB)
            return jnp.einsum('qcs,qcsd->qd', p_sel, gv) + p[:, n_sel * MB:] @ v_own

        outs.append(lax.map(attend, (q_i, kf[:, :i], vf[:, :i], sel, own, v_i, slope_f)))
    o = jnp.stack(outs, axis=1).reshape(B, H, nb * MB, hd)
    return o[:, :, :S]


def fox_attention(q, k, v, cum):
    B, H, S, hd = q.shape
    scale = hd ** -0.5
    QB = FOX_QBLOCK
    outs = []
    for i in range(S // QB):
        L = (i + 1) * QB
        qs = q[:, :, i * QB:L]
        lg = jnp.einsum('bhqd,bhkd->bhqk', qs, k[:, :, :L]).astype(jnp.float32) * scale
        lg = lg + cum[:, :, i * QB:L, None] - cum[:, :, None, :L]
        t_pos = i * QB + jnp.arange(QB)
        causal = t_pos[:, None] >= jnp.arange(L)[None, :]
        lg = jnp.where(causal[None, None], lg, -jnp.inf)
        p = jax.nn.softmax(lg, axis=-1).astype(v.dtype)
        outs.append(jnp.einsum('bhqk,bhkd->bhqd', p, v[:, :, :L]))
    return jnp.concatenate(outs, axis=2)


def moe_ffn(h, w_router, router_bias, w_in, w_out):
    B, S, D = h.shape
    N = B * S
    NK = N * TOP_K
    hf = h.reshape(N, D)
    s = jax.nn.sigmoid((hf @ w_router).astype(jnp.float32))
    sb = (s + router_bias.astype(jnp.float32)).reshape(N, N_GROUPS, EXPERTS_PER_GROUP)
    grp_score = jnp.sum(lax.top_k(sb, 2)[0], axis=-1)
    g = jnp.argmax(grp_score, axis=-1)
    in_grp = jnp.take_along_axis(sb, g[:, None, None], axis=1)[:, 0]
    _, loc = lax.top_k(in_grp, TOP_K)
    eid = g[:, None] * EXPERTS_PER_GROUP + loc
    wts = jnp.take_along_axis(s, eid, axis=1)
    wts = wts / jnp.sum(wts, axis=-1, keepdims=True)
    e_flat = eid.reshape(-1)
    w_flat = wts.reshape(-1)
    tok_flat = jnp.arange(NK, dtype=jnp.int32) // TOP_K
    order = jnp.argsort(e_flat, stable=True)
    e_sorted = e_flat[order]
    counts = jnp.bincount(e_flat, length=N_EXPERTS)
    RB = MOE_ROW_BLOCK
    padded = ((counts + RB - 1) // RB) * RB
    start = jnp.cumsum(counts) - counts
    start_pad = jnp.cumsum(padded) - padded
    dest = start_pad[e_sorted] + (jnp.arange(NK) - start[e_sorted])
    n_rows = ((NK + N_EXPERTS * (RB - 1) + RB - 1) // RB) * RB
    n_blk = n_rows // RB
    row_tok = jnp.full((n_rows,), N, dtype=jnp.int32).at[dest].set(tok_flat[order])
    row_w = jnp.zeros((n_rows,), jnp.float32).at[dest].set(w_flat[order])
    blk_exp = jnp.searchsorted(jnp.cumsum(padded), jnp.arange(n_blk) * RB, side='right')
    blk_exp = jnp.minimum(blk_exp, N_EXPERTS - 1)
    hpad = jnp.concatenate([hf, jnp.zeros((1, D), hf.dtype)], axis=0)
    xs = hpad[row_tok].reshape(n_blk, RB, D)

    def expert_block(args):
        xb, e = args
        a = xb @ w_in[e]
        return (jax.nn.silu(a[:, :D_FF_EXPERT]) * a[:, D_FF_EXPERT:]) @ w_out[e]

    ys = lax.map(expert_block, (xs, blk_exp)).reshape(n_rows, D)
    out = jax.ops.segment_sum(ys * row_w[:, None].astype(ys.dtype), row_tok, num_segments=N + 1)
    return out[:N].reshape(B, S, D)


def head_rms_norm(t, g):
    return rms_norm(t, g)


def shared_kv(x, cond, g_kv, w_mod_kv, b_mod_kv, w_kvf, b_f, g_k_b):
    B, S, _ = x.shape
    sh, sc = jnp.split(cond @ w_mod_kv + b_mod_kv, 2, axis=-1)
    z = modulate(rms_norm(x, g_kv), sh, sc)
    kvf = z @ w_kvf
    k = head_rms_norm(kvf[..., :ATT_WIDTH].reshape(B, S, N_HEADS, HEAD_DIM), g_k_b)
    v = kvf[..., ATT_WIDTH:2 * ATT_WIDTH].reshape(B, S, N_HEADS, HEAD_DIM)
    log_f = jax.nn.log_sigmoid(kvf[..., 2 * ATT_WIDTH:].astype(jnp.float32) + b_f.astype(jnp.float32))
    cum = jnp.cumsum(log_f, axis=1).transpose(0, 2, 1)
    return k.transpose(0, 2, 1, 3), v.transpose(0, 2, 1, 3), cum


def setup_inputs(seed: int = 0) -> dict:
    key = jax.random.key(seed)
    ks = jax.random.split(key, 24)
    D, W, H, HD, E, F = D_MODEL, ATT_WIDTH, N_HEADS, HEAD_DIM, N_EXPERTS, D_FF_EXPERT
    f32 = jnp.float32

    def nrm(k, shape, scale):
        return jax.random.normal(k, shape, f32) * scale

    def gain(k, shape):
        return 1.0 + 0.02 * jax.random.normal(k, shape, f32)

    return {
        "x": nrm(ks[0], (BATCH, SEQ, D), 1.0),
        "c": nrm(ks[1], (BATCH, D), 1.0),
        "g_attn": gain(ks[2], (DEPTH, D)),
        "g_ffn": gain(ks[3], (DEPTH, D)),
        "w_mod": nrm(ks[4], (DEPTH, D, N_MOD * D), MOD_SCALE * D ** -0.5),
        "b_mod": nrm(ks[5], (DEPTH, N_MOD * D), 0.02),
        "w_qkv_a": nrm(ks[6], (N_A, D, 3 * W), D ** -0.5),
        "g_q_a": gain(ks[7], (N_A, HD)),
        "g_k_a": gain(ks[8], (N_A, HD)),
        "w_o_a": nrm(ks[9], (N_A, W, D), W ** -0.5),
        "g_kv": gain(ks[10], (D,)),
        "w_mod_kv": nrm(ks[11], (D, 2 * D), MOD_SCALE * D ** -0.5),
        "b_mod_kv": nrm(ks[12], (2 * D,), 0.02),
        "w_kvf": nrm(ks[13], (D, 2 * W + H), D ** -0.5),
        "b_f": FORGET_BIAS_INIT + 0.5 * jax.random.normal(ks[14], (H,), f32),
        "g_k_b": gain(ks[15], (HD,)),
        "w_q_b": nrm(ks[16], (N_B, D, W), D ** -0.5),
        "g_q_b": gain(ks[17], (N_B, HD)),
        "w_o_b": nrm(ks[18], (N_B, W, D), W ** -0.5),
        "w_router": nrm(ks[19], (D, E), D ** -0.5),
        "router_bias": nrm(ks[20], (E,), 0.01),
        "w_in": nrm(ks[21], (DEPTH, E, D, 2 * F), D ** -0.5),
        "w_out": nrm(ks[22], (DEPTH, E, F, D), F ** -0.5),
    }


def reference(x, c, g_attn, g_ffn, w_mod, b_mod, w_qkv_a, g_q_a, g_k_a, w_o_a,
              g_kv, w_mod_kv, b_mod_kv, w_kvf, b_f, g_k_b, w_q_b, g_q_b, w_o_b,
              w_router, router_bias, w_in, w_out):
    B, S, D = x.shape
    slopes = alibi_slopes(N_HEADS)
    cond = jax.nn.silu(c)
    shared = None
    for l in range(DEPTH):
        mod = cond @ w_mod[l] + b_mod[l]
        sh_a, sc_a, gt_a, sh_m, sc_m, gt_m = jnp.split(mod, N_MOD, axis=-1)
        h = modulate(rms_norm(x, g_attn[l]), sh_a, sc_a)
        if l < N_A:
            qkv = (h @ w_qkv_a[l]).reshape(B, S, 3, N_HEADS, HEAD_DIM)
            q = head_rms_norm(qkv[:, :, 0], g_q_a[l]).transpose(0, 2, 1, 3)
            k = head_rms_norm(qkv[:, :, 1], g_k_a[l]).transpose(0, 2, 1, 3)
            v = qkv[:, :, 2].transpose(0, 2, 1, 3)
            o = moba_attention(q, k, v, slopes)
            w_o = w_o_a[l]
        else:
            j = l - N_A
            q = head_rms_norm((h @ w_q_b[j]).reshape(B, S, N_HEADS, HEAD_DIM), g_q_b[j]).transpose(0, 2, 1, 3)
            k_s, v_s, cum = shared
            o = fox_attention(q, k_s, v_s, cum)
            w_o = w_o_b[j]
        o = o.transpose(0, 2, 1, 3).reshape(B, S, ATT_WIDTH)
        x = x + gt_a[:, None, :] * (o @ w_o)
        h = modulate(rms_norm(x, g_ffn[l]), sh_m, sc_m)
        x = x + gt_m[:, None, :] * moe_ffn(h, w_router, router_bias, w_in[l], w_out[l])
        if l == N_A - 1:
            shared = shared_kv(x, cond, g_kv, w_mod_kv, b_mod_kv, w_kvf, b_f, g_k_b)
    return x
```

```python
import functools

import jax
import jax.numpy as jnp
from jax import lax
from jax.experimental import pallas as pl
from jax.experimental.pallas import tpu as pltpu

MOBA_BLOCK = 256
MOBA_TOPK = 3
N_GROUPS = 4
TOP_K = 2
EPS = 1e-6

LANE = 128
V7X_VMEM_BYTES = 64 * 1024 * 1024
VMEM_LIMIT = V7X_VMEM_BYTES - 8 * 1024 * 1024

MOE_ROW_BLOCK = 256
HIGHEST = lax.Precision.HIGHEST
BF16 = jnp.bfloat16
F32 = jnp.float32
NEG_INF = float("-inf")


def _params(*semantics):
    return pltpu.CompilerParams(dimension_semantics=semantics, vmem_limit_bytes=VMEM_LIMIT)


def _rms_mod(x, g, shift, scale):
    ms = jnp.mean(x * x, axis=-1, keepdims=True)
    y = x * lax.rsqrt(ms + EPS) * g
    return y * (1.0 + scale) + shift


def _mod_kernel(c_ref, w_ref, b_ref, o_ref):
    c = c_ref[...]
    cond = c * jax.nn.sigmoid(c)
    o_ref[0] = jnp.dot(cond, w_ref[0], precision=HIGHEST, preferred_element_type=F32) + b_ref[0]


def _mod_call(c_pad, w, b):
    n_l, d, m = w.shape
    rows = c_pad.shape[0]
    tn = min(1024, m)
    return pl.pallas_call(
        _mod_kernel,
        grid=(n_l, m // tn),
        in_specs=[
            pl.BlockSpec((rows, d), lambda l, j: (0, 0)),
            pl.BlockSpec((1, d, tn), lambda l, j: (l, 0, j)),
            pl.BlockSpec((1, 1, tn), lambda l, j: (l, 0, j)),
        ],
        out_specs=pl.BlockSpec((1, rows, tn), lambda l, j: (l, 0, j)),
        out_shape=jax.ShapeDtypeStruct((n_l, rows, m), F32),
        compiler_params=_params("parallel", "parallel"),
        name="adaln_mod",
    )(c_pad, w, b.reshape(n_l, 1, m))


def _norm_proj_kernel(x_ref, g_ref, sh_ref, sc_ref, w_ref, hg_ref, o_ref, h_scr, *, n_norm_tiles, n_col_tiles):
    j = pl.program_id(1)

    @pl.when(j == 0)
    def _():
        h_scr[...] = _rms_mod(x_ref[...], g_ref[...], sh_ref[0], sc_ref[0]).astype(BF16)

    y = jnp.dot(h_scr[...], w_ref[...], preferred_element_type=F32)

    def write_head_normed():
        hg = hg_ref[0]
        for hh in range(y.shape[1] // LANE):
            yh = y[:, hh * LANE:(hh + 1) * LANE]
            ms = jnp.mean(yh * yh, axis=-1, keepdims=True)
            o_ref[:, hh * LANE:(hh + 1) * LANE] = (yh * lax.rsqrt(ms + EPS) * hg).astype(o_ref.dtype)

    def write_plain():
        o_ref[...] = y.astype(o_ref.dtype)

    if n_norm_tiles == n_col_tiles:
        write_head_normed()
    else:
        pl.when(j < n_norm_tiles)(write_head_normed)
        pl.when(j >= n_norm_tiles)(write_plain)


def _norm_proj(x2, g, shift, scale, w, head_gain, n_norm_sections, seq):
    n, d = x2.shape
    m = w.shape[1]
    n_sections = head_gain.shape[0]
    sec_w = m // n_sections
    tm = min(512, seq)
    tn = min(512, sec_w)
    tiles_per_batch = seq // tm
    tiles_per_section = sec_w // tn
    n_col_tiles = m // tn
    kern = functools.partial(_norm_proj_kernel, n_norm_tiles=n_norm_sections * tiles_per_section,
                             n_col_tiles=n_col_tiles)
    return pl.pallas_call(
        kern,
        grid=(n // tm, n_col_tiles),
        in_specs=[
            pl.BlockSpec((tm, d), lambda i, j: (i, 0)),
            pl.BlockSpec((1, d), lambda i, j: (0, 0)),
            pl.BlockSpec((1, 1, d), lambda i, j: (i // tiles_per_batch, 0, 0)),
            pl.BlockSpec((1, 1, d), lambda i, j: (i // tiles_per_batch, 0, 0)),
            pl.BlockSpec((d, tn), lambda i, j: (0, j)),
            pl.BlockSpec((1, 1, LANE), lambda i, j: (j // tiles_per_section, 0, 0)),
        ],
        out_specs=pl.BlockSpec((tm, tn), lambda i, j: (i, j)),
        out_shape=jax.ShapeDtypeStruct((n, m), BF16),
        scratch_shapes=[pltpu.VMEM((tm, d), BF16)],
        compiler_params=_params("parallel", "arbitrary"),
        name="norm_proj",
    )(x2, g.reshape(1, d), shift, scale, w, head_gain)


def _qk(q, k):
    return lax.dot_general(q, k, (((1,), (1,)), ((), ())), preferred_element_type=F32)


def _softmax_step(s, v, m, l, acc):
    m_new = jnp.maximum(m, jnp.max(s, axis=-1, keepdims=True))
    alpha = jnp.exp(m - m_new)
    p = jnp.exp(s - m_new)
    l = alpha * l + jnp.sum(p, axis=-1, keepdims=True)
    acc = alpha * acc + jnp.dot(p.astype(v.dtype), v, preferred_element_type=F32)
    return m_new, l, acc


def _softmax_first(s, v):
    m = jnp.max(s, axis=-1, keepdims=True)
    p = jnp.exp(s - m)
    return m, jnp.sum(p, axis=-1, keepdims=True), jnp.dot(p.astype(v.dtype), v, preferred_element_type=F32)


def _moba_kernel(slopes_ref, q_ref, k_ref, v_ref, o_ref, kmean_scr, *, n_blocks, scale):
    h = pl.program_id(1)
    i = pl.program_id(2)
    mb = MOBA_BLOCK

    @pl.when(i == 0)
    def _():
        kmean_scr[...] = jnp.zeros_like(kmean_scr)
        for jb in range(n_blocks):
            kb = k_ref[0, jb * mb:(jb + 1) * mb, :].astype(F32)
            kmean_scr[jb:jb + 1, :] = jnp.mean(kb, axis=0, keepdims=True)

    slope = slopes_ref[h]
    q = q_ref[0]
    row = lax.broadcasted_iota(jnp.int32, (mb, mb), 0)
    col = lax.broadcasted_iota(jnp.int32, (mb, mb), 1)
    rel = (row - col).astype(F32)
    alibi = slope * rel

    k_own = k_ref[0, pl.ds(i * mb, mb), :]
    v_own = v_ref[0, pl.ds(i * mb, mb), :]
    s = _qk(q, k_own) * scale - alibi
    s = jnp.where(row >= col, s, NEG_INF)
    m, l, acc = _softmax_first(s, v_own)

    blk = lax.broadcasted_iota(jnp.int32, (mb, LANE), 1)
    gate = lax.dot_general(q.astype(F32), kmean_scr[...], (((1,), (1,)), ((), ())),
                           precision=HIGHEST, preferred_element_type=F32)
    avail = blk < i
    sel = jnp.zeros((mb, LANE), jnp.bool_)
    for _ in range(MOBA_TOPK):
        g = jnp.where(avail, gate, NEG_INF)
        gmax = jnp.max(g, axis=-1, keepdims=True)
        first = jnp.min(jnp.where(avail & (g == gmax), blk, LANE), axis=-1, keepdims=True)
        pick = blk == first
        sel = sel | pick
        avail = avail & jnp.logical_not(pick)
    sel_f = jnp.where(sel, 1.0, 0.0)

    def body(jb, carry):
        m, l, acc = carry
        kj = k_ref[0, pl.ds(jb * mb, mb), :]
        vj = v_ref[0, pl.ds(jb * mb, mb), :]
        chosen = jnp.sum(jnp.where(blk == jb, sel_f, 0.0), axis=-1, keepdims=True) > 0.0
        offset = slope * ((i - jb) * mb).astype(F32)
        s = _qk(q, kj) * scale - (alibi + offset)
        s = jnp.where(chosen, s, NEG_INF)
        return _softmax_step(s, vj, m, l, acc)

    m, l, acc = lax.fori_loop(0, i, body, (m, l, acc))
    o_ref[0] = (acc / l).astype(o_ref.dtype)


def _moba(qkv, slopes, n_heads):
    b, s, w3 = qkv.shape
    w = w3 // 3
    hd = w // n_heads
    nb = s // MOBA_BLOCK
    kern = functools.partial(_moba_kernel, n_blocks=nb, scale=hd ** -0.5)
    return pl.pallas_call(
        kern,
        grid=(b, n_heads, nb),
        in_specs=[
            pl.BlockSpec(memory_space=pltpu.SMEM),
            pl.BlockSpec((1, MOBA_BLOCK, hd), lambda bi, h, i: (bi, i, h)),
            pl.BlockSpec((1, s, hd), lambda bi, h, i: (bi, 0, n_heads + h)),
            pl.BlockSpec((1, s, hd), lambda bi, h, i: (bi, 0, 2 * n_heads + h)),
        ],
        out_specs=pl.BlockSpec((1, MOBA_BLOCK, hd), lambda bi, h, i: (bi, i, h)),
        out_shape=jax.ShapeDtypeStruct((b, s, w), BF16),
        scratch_shapes=[pltpu.VMEM((LANE, hd), F32)],
        compiler_params=_params("parallel", "parallel", "arbitrary"),
        name="moba_attention",
    )(slopes, qkv, qkv, qkv)


def _fox_kernel(q_ref, k_ref, v_ref, cc_ref, cr_ref, o_ref, *, tq, scale):
    i = pl.program_id(2)
    q = q_ref[0]
    cq = cc_ref[0, 0]
    row = lax.broadcasted_iota(jnp.int32, (tq, tq), 0)
    col = lax.broadcasted_iota(jnp.int32, (tq, tq), 1)

    def logits(jb):
        kj = k_ref[0, pl.ds(jb * tq, tq), :]
        return _qk(q, kj) * scale + cq - cr_ref[0, 0, jb]

    s = jnp.where(row >= col, logits(i), NEG_INF)
    m, l, acc = _softmax_first(s, v_ref[0, pl.ds(i * tq, tq), :])

    def body(jb, carry):
        m, l, acc = carry
        return _softmax_step(logits(jb), v_ref[0, pl.ds(jb * tq, tq), :], m, l, acc)

    m, l, acc = lax.fori_loop(0, i, body, (m, l, acc))
    o_ref[0] = (acc / l).astype(o_ref.dtype)


def _fox(q, kv, cum, n_heads):
    b, s, w = q.shape
    hd = w // n_heads
    tq = min(256, s)
    nt = s // tq
    kern = functools.partial(_fox_kernel, tq=tq, scale=hd ** -0.5)
    return pl.pallas_call(
        kern,
        grid=(b, n_heads, nt),
        in_specs=[
            pl.BlockSpec((1, tq, hd), lambda bi, h, i: (bi, i, h)),
            pl.BlockSpec((1, s, hd), lambda bi, h, i: (bi, 0, h)),
            pl.BlockSpec((1, s, hd), lambda bi, h, i: (bi, 0, n_heads + h)),
            pl.BlockSpec((1, 1, tq, 1), lambda bi, h, i: (bi, h, i, 0)),
            pl.BlockSpec((1, 1, nt, 1, tq), lambda bi, h, i: (bi, h, 0, 0, 0)),
        ],
        out_specs=pl.BlockSpec((1, tq, hd), lambda bi, h, i: (bi, i, h)),
        out_shape=jax.ShapeDtypeStruct((b, s, w), BF16),
        compiler_params=_params("parallel", "parallel", "arbitrary"),
        name="fox_attention",
    )(q, kv, kv, cum.reshape(b, n_heads, s, 1), cum.reshape(b, n_heads, nt, 1, tq))


def _decay_kernel(x_ref, g_ref, sh_ref, sc_ref, w_ref, b_ref, o_ref, carry_scr):
    t = pl.program_id(1)

    @pl.when(t == 0)
    def _():
        carry_scr[...] = jnp.zeros_like(carry_scr)

    z = _rms_mod(x_ref[...], g_ref[...], sh_ref[0], sc_ref[0])
    f = jnp.dot(z, w_ref[...], precision=HIGHEST, preferred_element_type=F32) + b_ref[...]
    log_f = jnp.minimum(f, 0.0) - jnp.log1p(jnp.exp(-jnp.abs(f)))
    tm = f.shape[0]
    lower = (lax.broadcasted_iota(jnp.int32, (tm, tm), 0) >= lax.broadcasted_iota(jnp.int32, (tm, tm), 1))
    cum = jnp.dot(jnp.where(lower, 1.0, 0.0), log_f, precision=HIGHEST, preferred_element_type=F32) + carry_scr[...]
    o_ref[0] = cum
    carry_scr[...] = cum[tm - 1:tm, :]


def _decay_cumsum(x2, g, shift, scale, w_f, b_f, batch, seq):
    n, d = x2.shape
    tm = min(512, seq)
    tiles = seq // tm
    return pl.pallas_call(
        _decay_kernel,
        grid=(batch, tiles),
        in_specs=[
            pl.BlockSpec((tm, d), lambda bi, t: (bi * tiles + t, 0)),
            pl.BlockSpec((1, d), lambda bi, t: (0, 0)),
            pl.BlockSpec((1, 1, d), lambda bi, t: (bi, 0, 0)),
            pl.BlockSpec((1, 1, d), lambda bi, t: (bi, 0, 0)),
            pl.BlockSpec((d, LANE), lambda bi, t: (0, 0)),
            pl.BlockSpec((1, LANE), lambda bi, t: (0, 0)),
        ],
        out_specs=pl.BlockSpec((1, tm, LANE), lambda bi, t: (bi, t, 0)),
        out_shape=jax.ShapeDtypeStruct((batch, seq, LANE), F32),
        scratch_shapes=[pltpu.VMEM((1, LANE), F32)],
        compiler_params=_params("parallel", "arbitrary"),
        name="decay_cumsum",
    )(x2, g.reshape(1, d), shift, scale, w_f, b_f)


def _out_proj_kernel(o_ref, w_ref, x_ref, gt_ref, y_ref):
    y = jnp.dot(o_ref[...], w_ref[...], preferred_element_type=F32)
    y_ref[...] = x_ref[...] + gt_ref[0] * y


def _out_proj(o2, w, x2, gate, seq):
    n, wdim = o2.shape
    d = w.shape[1]
    tm = min(512, seq)
    tn = min(1024, d)
    tiles_per_batch = seq // tm
    return pl.pallas_call(
        _out_proj_kernel,
        grid=(n // tm, d // tn),
        in_specs=[
            pl.BlockSpec((tm, wdim), lambda i, j: (i, 0)),
            pl.BlockSpec((wdim, tn), lambda i, j: (0, j)),
            pl.BlockSpec((tm, tn), lambda i, j: (i, j)),
            pl.BlockSpec((1, 1, tn), lambda i, j: (i // tiles_per_batch, 0, j)),
        ],
        out_specs=pl.BlockSpec((tm, tn), lambda i, j: (i, j)),
        out_shape=jax.ShapeDtypeStruct((n, d), F32),
        compiler_params=_params("parallel", "parallel"),
        name="out_proj",
    )(o2, w, x2, gate)


def _top2_sum(a, b, c, d):
    hi1, lo1 = jnp.maximum(a, b), jnp.minimum(a, b)
    hi2, lo2 = jnp.maximum(c, d), jnp.minimum(c, d)
    return jnp.maximum(hi1, hi2) + jnp.maximum(jnp.minimum(hi1, hi2), jnp.maximum(lo1, lo2))


def _pick(idx, rows):
    out = rows[-1]
    for k in range(len(rows) - 2, -1, -1):
        out = jnp.where(idx == k, rows[k], out)
    return out


def _router_kernel(x_ref, g_ref, sh_ref, sc_ref, wrt_ref, rb_ref, h_ref, eid_ref, wt_ref, *, n_experts):
    h = _rms_mod(x_ref[...], g_ref[...], sh_ref[0], sc_ref[0])
    for s in range(h.shape[1] // LANE):
        h_ref[:, s, :] = h[:, s * LANE:(s + 1) * LANE]

    logits = lax.dot_general(wrt_ref[...], h, (((1,), (1,)), ((), ())), precision=HIGHEST, preferred_element_type=F32)
    prob = jax.nn.sigmoid(logits)
    biased = prob + rb_ref[...]
    epg = n_experts // N_GROUPS
    p_rows = [prob[e:e + 1, :] for e in range(n_experts)]
    b_rows = [biased[e:e + 1, :] for e in range(n_experts)]

    scores = [_top2_sum(*b_rows[gi * epg:(gi + 1) * epg]) for gi in range(N_GROUPS)]
    grp = jnp.zeros_like(scores[0], dtype=jnp.int32)
    best = scores[0]
    for gi in range(1, N_GROUPS):
        better = scores[gi] > best
        grp = jnp.where(better, gi, grp)
        best = jnp.where(better, scores[gi], best)

    in_b = [_pick(grp, [b_rows[gi * epg + j] for gi in range(N_GROUPS)]) for j in range(epg)]
    in_p = [_pick(grp, [p_rows[gi * epg + j] for gi in range(N_GROUPS)]) for j in range(epg)]

    loc0 = jnp.zeros_like(grp)
    top = in_b[0]
    for j in range(1, epg):
        better = in_b[j] > top
        loc0 = jnp.where(better, j, loc0)
        top = jnp.where(better, in_b[j], top)
    loc1 = jnp.full_like(grp, -1)
    second = jnp.full_like(top, NEG_INF)
    for j in range(epg):
        better = (loc0 != j) & ((loc1 < 0) | (in_b[j] > second))
        loc1 = jnp.where(better, j, loc1)
        second = jnp.where(better, in_b[j], second)

    w0 = _pick(loc0, in_p)
    w1 = _pick(loc1, in_p)
    denom = w0 + w1
    pad_i = jnp.zeros((eid_ref.shape[0] - TOP_K, grp.shape[1]), jnp.int32)
    eid_ref[...] = jnp.concatenate([grp * epg + loc0, grp * epg + loc1, pad_i], axis=0)
    wt_ref[...] = jnp.concatenate([w0 / denom, w1 / denom, pad_i.astype(F32)], axis=0)


def _router(x2, g, shift, scale, w_router_t, router_bias, seq):
    n, d = x2.shape
    e = w_router_t.shape[0]
    tm = min(256, seq)
    tiles_per_batch = seq // tm
    kern = functools.partial(_router_kernel, n_experts=e)
    return pl.pallas_call(
        kern,
        grid=(n // tm,),
        in_specs=[
            pl.BlockSpec((tm, d), lambda i: (i, 0)),
            pl.BlockSpec((1, d), lambda i: (0, 0)),
            pl.BlockSpec((1, 1, d), lambda i: (i // tiles_per_batch, 0, 0)),
            pl.BlockSpec((1, 1, d), lambda i: (i // tiles_per_batch, 0, 0)),
            pl.BlockSpec((e, d), lambda i: (0, 0)),
            pl.BlockSpec((e, 1), lambda i: (0, 0)),
        ],
        out_specs=[
            pl.BlockSpec((tm, d // LANE, LANE), lambda i: (i, 0, 0)),
            pl.BlockSpec((8, tm), lambda i: (0, i)),
            pl.BlockSpec((8, tm), lambda i: (0, i)),
        ],
        out_shape=[
            jax.ShapeDtypeStruct((n, d // LANE, LANE), F32),
            jax.ShapeDtypeStruct((8, n), jnp.int32),
            jax.ShapeDtypeStruct((8, n), F32),
        ],
        compiler_params=_params("parallel"),
        name="moe_router",
    )(x2, g.reshape(1, d), shift, scale, w_router_t, router_bias.reshape(e, 1))


def _expert_kernel(row_tok_ref, blk_exp_ref, n_used_ref, h_hbm, w_in_ref, w_out_ref, y_ref, buf, xs_scr, sem, *, d_ff):
    i = pl.program_id(0)
    tm = buf.shape[0]

    @pl.when(i < n_used_ref[0])
    def _():
        def issue(r, carry):
            pltpu.make_async_copy(h_hbm.at[row_tok_ref[i * tm + r]], buf.at[r], sem).start()
            return carry

        lax.fori_loop(0, tm, issue, 0, unroll=8)
        pltpu.make_async_copy(h_hbm.at[pl.ds(0, tm)], buf, sem).wait()
        for s in range(buf.shape[1]):
            xs_scr[:, s * LANE:(s + 1) * LANE] = buf[:, s, :].astype(BF16)
        a = jnp.dot(xs_scr[...], w_in_ref[0], preferred_element_type=F32)
        gate, up = a[:, :d_ff], a[:, d_ff:]
        act = (gate * jax.nn.sigmoid(gate) * up).astype(BF16)
        y = jnp.dot(act, w_out_ref[0], preferred_element_type=F32)
        for s in range(buf.shape[1]):
            y_ref[:, s, :] = y[:, s * LANE:(s + 1) * LANE]

    @pl.when(i >= n_used_ref[0])
    def _():
        y_ref[...] = jnp.zeros_like(y_ref)


def _experts(h3, row_tok, blk_exp, n_used, w_in, w_out):
    n, ds, _ = h3.shape
    d = ds * LANE
    e, _, f2 = w_in.shape
    d_ff = f2 // 2
    n_rows = row_tok.shape[0]
    tm = MOE_ROW_BLOCK
    n_blk = n_rows // tm
    kern = functools.partial(_expert_kernel, d_ff=d_ff)
    grid_spec = pltpu.PrefetchScalarGridSpec(
        num_scalar_prefetch=3,
        grid=(n_blk,),
        in_specs=[
            pl.BlockSpec(memory_space=pl.ANY),
            pl.BlockSpec((1, d, f2), lambda i, rt, be, nu: (be[i], 0, 0)),
            pl.BlockSpec((1, d_ff, d), lambda i, rt, be, nu: (be[i], 0, 0)),
        ],
        out_specs=pl.BlockSpec((tm, ds, LANE), lambda i, rt, be, nu: (i, 0, 0)),
        scratch_shapes=[
            pltpu.VMEM((tm, ds, LANE), F32),
            pltpu.VMEM((tm, d), BF16),
            pltpu.SemaphoreType.DMA(()),
        ],
    )
    return pl.pallas_call(
        kern,
        grid_spec=grid_spec,
        out_shape=jax.ShapeDtypeStruct((n_rows, ds, LANE), F32),
        compiler_params=_params("arbitrary"),
        name="moe_experts",
    )(row_tok, blk_exp, n_used, h3, w_in, w_out)


def _combine_kernel(pos_ref, y_hbm, x_ref, wt_ref, gt_ref, o_ref, buf0, buf1, sem):
    i = pl.program_id(0)
    tm = buf0.shape[0]

    def issue(r, carry):
        t = i * tm + r
        pltpu.make_async_copy(y_hbm.at[pos_ref[TOP_K * t]], buf0.at[r], sem.at[0]).start()
        pltpu.make_async_copy(y_hbm.at[pos_ref[TOP_K * t + 1]], buf1.at[r], sem.at[1]).start()
        return carry

    lax.fori_loop(0, tm, issue, 0, unroll=8)
    pltpu.make_async_copy(y_hbm.at[pl.ds(0, tm)], buf0, sem.at[0]).wait()
    pltpu.make_async_copy(y_hbm.at[pl.ds(0, tm)], buf1, sem.at[1]).wait()
    w0 = wt_ref[:, 0:1]
    w1 = wt_ref[:, 1:2]
    for s in range(buf0.shape[1]):
        cols = slice(s * LANE, (s + 1) * LANE)
        moe = buf0[:, s, :] * w0 + buf1[:, s, :] * w1
        o_ref[:, cols] = x_ref[:, cols] + gt_ref[0][:, cols] * moe


def _combine(y3, pos, x2, wts, gate, seq):
    n, d = x2.shape
    ds = d // LANE
    tm = min(256, seq)
    tiles_per_batch = seq // tm
    grid_spec = pltpu.PrefetchScalarGridSpec(
        num_scalar_prefetch=1,
        grid=(n // tm,),
        in_specs=[
            pl.BlockSpec(memory_space=pl.ANY),
            pl.BlockSpec((tm, d), lambda i, p: (i, 0)),
            pl.BlockSpec((tm, TOP_K), lambda i, p: (i, 0)),
            pl.BlockSpec((1, 1, d), lambda i, p: (i // tiles_per_batch, 0, 0)),
        ],
        out_specs=pl.BlockSpec((tm, d), lambda i, p: (i, 0)),
        scratch_shapes=[
            pltpu.VMEM((tm, ds, LANE), F32),
            pltpu.VMEM((tm, ds, LANE), F32),
            pltpu.SemaphoreType.DMA((2,)),
        ],
    )
    return pl.pallas_call(
        _combine_kernel,
        grid_spec=grid_spec,
        out_shape=jax.ShapeDtypeStruct((n, d), F32),
        compiler_params=_params("arbitrary"),
        name="moe_combine",
    )(pos, y3, x2, wts, gate)


def _moe_layer(x2, g, shift, scale, gate, w_router_t, router_bias, w_in, w_out, seq):
    n, d = x2.shape
    e = w_router_t.shape[0]
    h3, eid8, wt8 = _router(x2, g, shift, scale, w_router_t, router_bias, seq)
    eid = eid8[:TOP_K].T
    wts = wt8[:TOP_K].T

    rb = MOE_ROW_BLOCK
    nk = n * TOP_K
    e_flat = eid.reshape(nk)
    one_hot = (e_flat[:, None] == jnp.arange(e, dtype=jnp.int32)[None, :]).astype(jnp.int32)
    running = jnp.cumsum(one_hot, axis=0)
    counts = running[-1]
    rank = jnp.sum(running * one_hot, axis=1) - 1
    padded = ((counts + rb - 1) // rb) * rb
    ends = jnp.cumsum(padded)
    pos = ((ends - padded)[e_flat] + rank).astype(jnp.int32)
    n_rows = ((nk + e * (rb - 1) + rb - 1) // rb) * rb
    n_blk = n_rows // rb
    tok = jnp.arange(nk, dtype=jnp.int32) // TOP_K
    row_tok = jnp.zeros((n_rows,), jnp.int32).at[pos].set(tok)
    blk_exp = jnp.searchsorted(ends, jnp.arange(n_blk, dtype=jnp.int32) * rb, side="right")
    blk_exp = jnp.minimum(blk_exp, e - 1).astype(jnp.int32)
    n_used = (ends[-1] // rb).astype(jnp.int32).reshape(1)

    y3 = _experts(h3, row_tok, blk_exp, n_used, w_in, w_out)
    return _combine(y3, pos, x2, wts, gate, seq)


def kernel(x, c, g_attn, g_ffn, w_mod, b_mod, w_qkv_a, g_q_a, g_k_a, w_o_a, g_kv, w_mod_kv, b_mod_kv, w_kvf, b_f,
           g_k_b, w_q_b, g_q_b, w_o_b, w_router, router_bias, w_in, w_out):
    b, s, d = x.shape
    n = b * s
    depth = g_attn.shape[0]
    n_a = w_qkv_a.shape[0]
    n_heads = b_f.shape[0]
    hd = g_q_a.shape[-1]
    w = n_heads * hd
    assert hd == LANE and s % MOBA_BLOCK == 0 and d % LANE == 0

    c_pad = jnp.pad(c, ((0, 8 - b), (0, 0)))
    mod = _mod_call(c_pad, w_mod, b_mod)[:, :b]
    mod_kv = _mod_call(c_pad, w_mod_kv[None], b_mod_kv[None])[0, :b]
    slopes = jnp.exp2(-8.0 * jnp.arange(1, n_heads + 1, dtype=F32) / n_heads)
    w_router_t = w_router.T
    ones_hd = jnp.ones((hd,), F32)

    x2 = x.reshape(n, d)
    kv = cum = None
    for l in range(depth):
        sh_a, sc_a, gt_a, sh_m, sc_m, gt_m = (mod[l][:, k * d:(k + 1) * d].reshape(b, 1, d) for k in range(6))
        if l < n_a:
            gains = jnp.stack([g_q_a[l], g_k_a[l], ones_hd]).reshape(3, 1, hd)
            qkv = _norm_proj(x2, g_attn[l], sh_a, sc_a, w_qkv_a[l].astype(BF16), gains, 2, s)
            o = _moba(qkv.reshape(b, s, 3 * w), slopes, n_heads)
            w_o = w_o_a[l]
        else:
            j = l - n_a
            q = _norm_proj(x2, g_attn[l], sh_a, sc_a, w_q_b[j].astype(BF16), g_q_b[j].reshape(1, 1, hd), 1, s)
            o = _fox(q.reshape(b, s, w), kv, cum, n_heads)
            w_o = w_o_b[j]
        x2 = _out_proj(o.reshape(n, w), w_o.astype(BF16), x2, gt_a, s)
        x2 = _moe_layer(x2, g_ffn[l], sh_m, sc_m, gt_m, w_router_t, router_bias,
                        w_in[l].astype(BF16), w_out[l].astype(BF16), s)
        if l == n_a - 1:
            sh_kv = mod_kv[:, :d].reshape(b, 1, d)
            sc_kv = mod_kv[:, d:].reshape(b, 1, d)
            gains = jnp.stack([g_k_b, ones_hd]).reshape(2, 1, hd)
            kv = _norm_proj(x2, g_kv, sh_kv, sc_kv, w_kvf[:, :2 * w].astype(BF16), gains, 1, s).reshape(b, s, 2 * w)
            w_f = jnp.pad(w_kvf[:, 2 * w:], ((0, 0), (0, LANE - n_heads)))
            b_f_pad = jnp.pad(b_f, (0, LANE - n_heads)).reshape(1, LANE)
            cum3 = _decay_cumsum(x2, g_kv, sh_kv, sc_kv, w_f, b_f_pad, b, s)
            cum = cum3[:, :, :n_heads].transpose(0, 2, 1)
    return x2.reshape(b, s, d)
```

```python
import functools
import math

import jax
import jax.numpy as jnp
from jax import lax
from jax.experimental import pallas as pl
from jax.experimental.pallas import tpu as pltpu

MOBA_BLOCK = 256
MOBA_TOPK = 3
N_GROUPS = 4
TOP_K = 2
EPS = 1e-6

LANE = 128
V7X_VMEM_BYTES = 64 * 1024 * 1024
VMEM_LIMIT = V7X_VMEM_BYTES - 8 * 1024 * 1024

MOE_ROW_BLOCK = 256
ATTN_BLOCK = 256
ATTN_HEADS = 2
HIGHEST = lax.Precision.HIGHEST
BF16 = jnp.bfloat16
F32 = jnp.float32
NEG_INF = float("-inf")
MASKED = -1e30
LOG2E = math.log2(math.e)


def _params(*semantics):
    return pltpu.CompilerParams(dimension_semantics=semantics, vmem_limit_bytes=VMEM_LIMIT)


def _rms_mod(x, g, shift, scale):
    ms = jnp.mean(x * x, axis=-1, keepdims=True)
    y = x * lax.rsqrt(ms + EPS) * g
    return y * (1.0 + scale) + shift


def _bf16_pieces(x):
    hi = x.astype(BF16).astype(F32)
    mid = (x - hi).astype(BF16).astype(F32)
    lo = (x - hi - mid).astype(BF16).astype(F32)
    return hi, mid, lo


def _mod_kernel(c_ref, w_ref, b_ref, o_ref):
    c = c_ref[...]
    cond = c * jax.nn.sigmoid(c)
    o_ref[0] = jnp.dot(cond, w_ref[0], precision=HIGHEST, preferred_element_type=F32) + b_ref[0]


def _mod_call(c_pad, w, b):
    n_l, d, m = w.shape
    rows = c_pad.shape[0]
    tn = min(1024, m)
    return pl.pallas_call(
        _mod_kernel,
        grid=(n_l, m // tn),
        in_specs=[
            pl.BlockSpec((rows, d), lambda l, j: (0, 0)),
            pl.BlockSpec((1, d, tn), lambda l, j: (l, 0, j)),
            pl.BlockSpec((1, 1, tn), lambda l, j: (l, 0, j)),
        ],
        out_specs=pl.BlockSpec((1, rows, tn), lambda l, j: (l, 0, j)),
        out_shape=jax.ShapeDtypeStruct((n_l, rows, m), F32),
        compiler_params=_params("parallel", "parallel"),
        name="adaln_mod",
    )(c_pad, w, b.reshape(n_l, 1, m))


def _norm_proj_kernel(x_ref, g_ref, sh_ref, sc_ref, w_ref, hg_ref, o_ref, h_scr, *, n_norm_tiles, n_col_tiles):
    j = pl.program_id(1)

    @pl.when(j == 0)
    def _():
        h_scr[...] = _rms_mod(x_ref[...], g_ref[...], sh_ref[0], sc_ref[0]).astype(BF16)

    y = jnp.dot(h_scr[...], w_ref[...], preferred_element_type=F32)

    def write_head_normed():
        hg = hg_ref[0]
        for hh in range(y.shape[1] // LANE):
            yh = y[:, hh * LANE:(hh + 1) * LANE]
            ms = jnp.mean(yh * yh, axis=-1, keepdims=True)
            o_ref[:, hh * LANE:(hh + 1) * LANE] = (yh * lax.rsqrt(ms + EPS) * hg).astype(o_ref.dtype)

    def write_plain():
        o_ref[...] = y.astype(o_ref.dtype)

    if n_norm_tiles == n_col_tiles:
        write_head_normed()
    else:
        pl.when(j < n_norm_tiles)(write_head_normed)
        pl.when(j >= n_norm_tiles)(write_plain)


def _norm_proj(x2, g, shift, scale, w, head_gain, n_norm_sections, seq):
    n, d = x2.shape
    m = w.shape[1]
    n_sections = head_gain.shape[0]
    sec_w = m // n_sections
    tm = min(512, seq)
    tn = min(512, sec_w)
    tiles_per_batch = seq // tm
    tiles_per_section = sec_w // tn
    n_col_tiles = m // tn
    kern = functools.partial(_norm_proj_kernel, n_norm_tiles=n_norm_sections * tiles_per_section,
                             n_col_tiles=n_col_tiles)
    return pl.pallas_call(
        kern,
        grid=(n // tm, n_col_tiles),
        in_specs=[
            pl.BlockSpec((tm, d), lambda i, j: (i, 0)),
            pl.BlockSpec((1, d), lambda i, j: (0, 0)),
            pl.BlockSpec((1, 1, d), lambda i, j: (i // tiles_per_batch, 0, 0)),
            pl.BlockSpec((1, 1, d), lambda i, j: (i // tiles_per_batch, 0, 0)),
            pl.BlockSpec((d, tn), lambda i, j: (0, j)),
            pl.BlockSpec((1, 1, LANE), lambda i, j: (j // tiles_per_section, 0, 0)),
        ],
        out_specs=pl.BlockSpec((tm, tn), lambda i, j: (i, j)),
        out_shape=jax.ShapeDtypeStruct((n, m), BF16),
        scratch_shapes=[pltpu.VMEM((tm, d), BF16)],
        compiler_params=_params("parallel", "arbitrary"),
        name="norm_proj",
    )(x2, g.reshape(1, d), shift, scale, w, head_gain)


def _flash(i, q_wides, kp_scr, vp_scr, s_scr):
    tq = ATTN_BLOCK
    chunk = 2 * tq
    heads = range(len(q_wides))
    n_chunks = lax.shift_right_logical(i, 1)

    def causal(width):
        row = lax.broadcasted_iota(jnp.int32, (tq, width), 0)
        col = lax.broadcasted_iota(jnp.int32, (tq, width), 1)
        return row >= col - (width - tq)

    def scores(hh, start, size):
        return lax.dot_general(q_wides[hh], kp_scr[hh, pl.ds(start, size), :], (((1,), (1,)), ((), ())),
                               preferred_element_type=F32)

    def weighted_values(hh, p, start, size):
        return jnp.dot(p.astype(BF16), vp_scr[hh, pl.ds(start, size), :], preferred_element_type=F32)

    def diagonal(start, size):
        for hh in heads:
            s_scr[0, hh] = scores(hh, 0, chunk)
        ms, accs = [], []
        for hh in heads:
            s = jnp.where(causal(size), scores(hh, start, size), MASKED)
            m = jnp.max(s, axis=-1, keepdims=True)
            ms.append(m)
            accs.append(weighted_values(hh, jnp.exp2(s - m), start, size))
        return tuple(ms), tuple(accs)

    ms, accs = lax.cond(lax.bitwise_and(i, 1) == 1,
                        lambda: diagonal(pl.multiple_of((i - 1) * tq, tq), chunk),
                        lambda: diagonal(pl.multiple_of(i * tq, tq), tq))

    def body(jj, carry):
        ms, accs = carry
        slot = lax.bitwise_and(jj, 1)
        start = pl.multiple_of(jj * chunk, chunk)
        ahead = pl.multiple_of(jnp.minimum(jj + 1, n_chunks - 1) * chunk, chunk)
        s_now = [s_scr[slot, hh] for hh in heads]
        for hh in heads:
            s_scr[1 - slot, hh] = scores(hh, ahead, chunk)
        new_ms, new_accs = [], []
        for hh in heads:
            m_new = jnp.maximum(ms[hh], jnp.max(s_now[hh], axis=-1, keepdims=True))
            p = jnp.exp2(s_now[hh] - m_new)
            new_ms.append(m_new)
            new_accs.append(jnp.exp2(ms[hh] - m_new) * accs[hh] + weighted_values(hh, p, start, chunk))
        return tuple(new_ms), tuple(new_accs)

    _, accs = lax.fori_loop(0, n_chunks, body, (ms, accs))
    return accs


def _head_cols(hh):
    return slice(hh * LANE, (hh + 1) * LANE)


def _widen_values(v_ref, vp_scr, n_blocks):
    tq = ATTN_BLOCK
    ones_col = jnp.where(lax.broadcasted_iota(jnp.int32, (tq, LANE), 1) == 0, 1.0, 0.0).astype(BF16)

    def fill(jb, carry):
        rows = pl.ds(pl.multiple_of(jb * tq, tq), tq)
        for hh in range(ATTN_HEADS):
            vp_scr[hh, rows, 0:LANE] = v_ref[0, rows, _head_cols(hh)]
            vp_scr[hh, rows, LANE:2 * LANE] = ones_col
        return carry

    lax.fori_loop(0, n_blocks, fill, 0)


def _finish(accs, o_ref):
    for hh, acc in enumerate(accs):
        o_ref[0, :, _head_cols(hh)] = (acc[:, :LANE] / acc[:, LANE:LANE + 1]).astype(o_ref.dtype)


def _attn_scratch(seq):
    return [
        pltpu.VMEM((ATTN_HEADS, seq, 2 * LANE), BF16),
        pltpu.VMEM((ATTN_HEADS, seq, 2 * LANE), BF16),
        pltpu.VMEM((2, ATTN_HEADS, ATTN_BLOCK, 2 * ATTN_BLOCK), F32),
    ]


_MOBA_MAX_BLOCKS = 16


def _moba_kernel(slopes_ref, q_ref, k_ref, v_ref, o_ref, kmean_scr, kp_scr, vp_scr, s_scr, *, n_blocks):
    hp = pl.program_id(1)
    i = pl.program_id(2)
    mb = MOBA_BLOCK
    lane = lax.broadcasted_iota(jnp.int32, (mb, LANE), 1)
    pos = lax.broadcasted_iota(jnp.int32, (mb, LANE), 0).astype(F32)

    def slope_lanes(hh, base, sign, init):
        out = init
        for p in range(3):
            piece = slopes_ref[3 * (ATTN_HEADS * hp + hh) + p]
            out = jnp.where((lane == base + p) | (lane == base + 3 + p), sign * piece, out)
        return out

    @pl.when(i == 0)
    def _():
        kmean_scr[...] = jnp.zeros_like(kmean_scr)
        _widen_values(v_ref, vp_scr, n_blocks)

        def fill(jb, carry):
            rows = pl.ds(pl.multiple_of(jb * mb, mb), mb)
            shared = jnp.where(lane == jb, 1.0, 0.0)
            shared = jnp.where((lane >= 22) & (lane < 25), pos, shared)
            shared = jnp.where((lane >= 25) & (lane < 28), jnp.asarray(jb * mb, F32), shared)
            for hh in range(ATTN_HEADS):
                kb = k_ref[0, rows, _head_cols(hh)]
                kmean_scr[hh, pl.ds(jb, 1), :] = jnp.mean(kb.astype(F32), axis=0, keepdims=True)
                kp_scr[hh, rows, 0:LANE] = kb
                kp_scr[hh, rows, LANE:2 * LANE] = slope_lanes(hh, 16, -1.0, shared).astype(BF16)
            return carry

        lax.fori_loop(0, n_blocks, fill, 0)

    q_wides = []
    blk = lax.broadcasted_iota(jnp.int32, (_MOBA_MAX_BLOCKS, mb), 0)
    for hh in range(ATTN_HEADS):
        q = q_ref[0, :, _head_cols(hh)]
        gate = lax.dot_general(kmean_scr[hh], q.astype(F32), (((1,), (1,)), ((), ())),
                               precision=HIGHEST, preferred_element_type=F32)
        avail = blk < i
        open_blk = blk == i
        for _ in range(MOBA_TOPK):
            g = jnp.where(avail, gate, NEG_INF)
            gmax = jnp.max(g, axis=0, keepdims=True)
            first = jnp.min(jnp.where(avail & (g == gmax), blk, _MOBA_MAX_BLOCKS), axis=0, keepdims=True)
            pick = blk == first
            open_blk = open_blk | pick
            avail = avail & jnp.logical_not(pick)
        mask_t = jnp.concatenate([jnp.where(open_blk, 0.0, MASKED),
                                  jnp.zeros((LANE - _MOBA_MAX_BLOCKS, mb), F32)], axis=0)

        aug = mask_t.T
        aug = jnp.where((lane >= 16) & (lane < 19), pos, aug)
        aug = jnp.where((lane >= 19) & (lane < 22), jnp.asarray(i * mb, F32), aug)
        aug = slope_lanes(hh, 22, 1.0, aug)
        q_wides.append(jnp.concatenate([q, aug.astype(BF16)], axis=1))
    _finish(_flash(i, q_wides, kp_scr, vp_scr, s_scr), o_ref)


def _moba(qkv, slope_pieces, n_heads):
    b, s, w3 = qkv.shape
    w = w3 // 3
    hd = w // n_heads
    nb = s // MOBA_BLOCK
    assert nb <= _MOBA_MAX_BLOCKS and hd == LANE and MOBA_BLOCK == ATTN_BLOCK
    assert n_heads % ATTN_HEADS == 0 and s >= 2 * ATTN_BLOCK
    kern = functools.partial(_moba_kernel, n_blocks=nb)
    groups = n_heads // ATTN_HEADS
    gw = ATTN_HEADS * hd
    return pl.pallas_call(
        kern,
        grid=(b, groups, nb),
        in_specs=[
            pl.BlockSpec(memory_space=pltpu.SMEM),
            pl.BlockSpec((1, MOBA_BLOCK, gw), lambda bi, hp, i: (bi, i, hp)),
            pl.BlockSpec((1, s, gw), lambda bi, hp, i: (bi, 0, groups + hp)),
            pl.BlockSpec((1, s, gw), lambda bi, hp, i: (bi, 0, 2 * groups + hp)),
        ],
        out_specs=pl.BlockSpec((1, MOBA_BLOCK, gw), lambda bi, hp, i: (bi, i, hp)),
        out_shape=jax.ShapeDtypeStruct((b, s, w), BF16),
        scratch_shapes=[pltpu.VMEM((ATTN_HEADS, _MOBA_MAX_BLOCKS, hd), F32)] + _attn_scratch(s),
        compiler_params=_params("parallel", "parallel", "arbitrary"),
        name="moba_attention",
    )(slope_pieces, qkv, qkv, qkv)


def _fox_kernel(q_ref, k_ref, v_ref, cum_ref, o_ref, qa_scr, kp_scr, vp_scr, s_scr, *, n_blocks):
    i = pl.program_id(2)
    tq = ATTN_BLOCK
    lane = lax.broadcasted_iota(jnp.int32, (tq, LANE), 1)

    @pl.when(i == 0)
    def _():
        _widen_values(v_ref, vp_scr, n_blocks)

        def fill(jb, carry):
            rows = pl.ds(pl.multiple_of(jb * tq, tq), tq)
            for hh in range(ATTN_HEADS):
                q_side = jnp.where((lane >= 3) & (lane < 6), 1.0, 0.0)
                k_side = jnp.where(lane < 3, 1.0, 0.0)
                for p, piece in enumerate(_bf16_pieces(cum_ref[0, hh, rows, :] * LOG2E)):
                    q_side = jnp.where(lane == p, piece, q_side)
                    k_side = jnp.where(lane == 3 + p, -piece, k_side)
                qa_scr[hh, rows, :] = q_side.astype(BF16)
                kp_scr[hh, rows, 0:LANE] = k_ref[0, rows, _head_cols(hh)]
                kp_scr[hh, rows, LANE:2 * LANE] = k_side.astype(BF16)
            return carry

        lax.fori_loop(0, n_blocks, fill, 0)

    rows = pl.ds(pl.multiple_of(i * tq, tq), tq)
    q_wides = [jnp.concatenate([q_ref[0, :, _head_cols(hh)], qa_scr[hh, rows, :]], axis=1)
               for hh in range(ATTN_HEADS)]
    _finish(_flash(i, q_wides, kp_scr, vp_scr, s_scr), o_ref)


def _fox(q, kv, cum, n_heads):
    b, s, w = q.shape
    hd = w // n_heads
    tq = ATTN_BLOCK
    nt = s // tq
    assert hd == LANE and s % tq == 0 and n_heads % ATTN_HEADS == 0 and s >= 2 * tq
    kern = functools.partial(_fox_kernel, n_blocks=nt)
    groups = n_heads // ATTN_HEADS
    gw = ATTN_HEADS * hd
    return pl.pallas_call(
        kern,
        grid=(b, groups, nt),
        in_specs=[
            pl.BlockSpec((1, tq, gw), lambda bi, hp, i: (bi, i, hp)),
            pl.BlockSpec((1, s, gw), lambda bi, hp, i: (bi, 0, hp)),
            pl.BlockSpec((1, s, gw), lambda bi, hp, i: (bi, 0, groups + hp)),
            pl.BlockSpec((1, ATTN_HEADS, s, 1), lambda bi, hp, i: (bi, hp, 0, 0)),
        ],
        out_specs=pl.BlockSpec((1, tq, gw), lambda bi, hp, i: (bi, i, hp)),
        out_shape=jax.ShapeDtypeStruct((b, s, w), BF16),
        scratch_shapes=[pltpu.VMEM((ATTN_HEADS, s, LANE), BF16)] + _attn_scratch(s),
        compiler_params=_params("parallel", "parallel", "arbitrary"),
        name="fox_attention",
    )(q, kv, kv, cum.reshape(b, n_heads, s, 1))


def _decay_kernel(x_ref, g_ref, sh_ref, sc_ref, w_ref, b_ref, o_ref, carry_scr):
    t = pl.program_id(1)

    @pl.when(t == 0)
    def _():
        carry_scr[...] = jnp.zeros_like(carry_scr)

    z = _rms_mod(x_ref[...], g_ref[...], sh_ref[0], sc_ref[0])
    f = jnp.dot(z, w_ref[...], precision=HIGHEST, preferred_element_type=F32) + b_ref[...]
    log_f = jnp.minimum(f, 0.0) - jnp.log1p(jnp.exp(-jnp.abs(f)))
    tm = f.shape[0]
    lower = (lax.broadcasted_iota(jnp.int32, (tm, tm), 0) >= lax.broadcasted_iota(jnp.int32, (tm, tm), 1))
    cum = jnp.dot(jnp.where(lower, 1.0, 0.0), log_f, precision=HIGHEST, preferred_element_type=F32) + carry_scr[...]
    o_ref[0] = cum
    carry_scr[...] = cum[tm - 1:tm, :]


def _decay_cumsum(x2, g, shift, scale, w_f, b_f, batch, seq):
    n, d = x2.shape
    tm = min(512, seq)
    tiles = seq // tm
    return pl.pallas_call(
        _decay_kernel,
        grid=(batch, tiles),
        in_specs=[
            pl.BlockSpec((tm, d), lambda bi, t: (bi * tiles + t, 0)),
            pl.BlockSpec((1, d), lambda bi, t: (0, 0)),
            pl.BlockSpec((1, 1, d), lambda bi, t: (bi, 0, 0)),
            pl.BlockSpec((1, 1, d), lambda bi, t: (bi, 0, 0)),
            pl.BlockSpec((d, LANE), lambda bi, t: (0, 0)),
            pl.BlockSpec((1, LANE), lambda bi, t: (0, 0)),
        ],
        out_specs=pl.BlockSpec((1, tm, LANE), lambda bi, t: (bi, t, 0)),
        out_shape=jax.ShapeDtypeStruct((batch, seq, LANE), F32),
        scratch_shapes=[pltpu.VMEM((1, LANE), F32)],
        compiler_params=_params("parallel", "arbitrary"),
        name="decay_cumsum",
    )(x2, g.reshape(1, d), shift, scale, w_f, b_f)


def _out_proj_kernel(o_ref, w_ref, x_ref, gt_ref, y_ref):
    y = jnp.dot(o_ref[...], w_ref[...], preferred_element_type=F32)
    y_ref[...] = x_ref[...] + gt_ref[0] * y


def _out_proj(o2, w, x2, gate, seq):
    n, wdim = o2.shape
    d = w.shape[1]
    tm = min(512, seq)
    tn = min(1024, d)
    tiles_per_batch = seq // tm
    return pl.pallas_call(
        _out_proj_kernel,
        grid=(n // tm, d // tn),
        in_specs=[
            pl.BlockSpec((tm, wdim), lambda i, j: (i, 0)),
            pl.BlockSpec((wdim, tn), lambda i, j: (0, j)),
            pl.BlockSpec((tm, tn), lambda i, j: (i, j)),
            pl.BlockSpec((1, 1, tn), lambda i, j: (i // tiles_per_batch, 0, j)),
        ],
        out_specs=pl.BlockSpec((tm, tn), lambda i, j: (i, j)),
        out_shape=jax.ShapeDtypeStruct((n, d), F32),
        compiler_params=_params("parallel", "parallel"),
        name="out_proj",
    )(o2, w, x2, gate)


def _top2_sum(a, b, c, d):
    hi1, lo1 = jnp.maximum(a, b), jnp.minimum(a, b)
    hi2, lo2 = jnp.maximum(c, d), jnp.minimum(c, d)
    return jnp.maximum(hi1, hi2) + jnp.maximum(jnp.minimum(hi1, hi2), jnp.maximum(lo1, lo2))


def _pick(idx, rows):
    out = rows[-1]
    for k in range(len(rows) - 2, -1, -1):
        out = jnp.where(idx == k, rows[k], out)
    return out


def _store_token_major(ref, mat):
    rows, d = mat.shape
    ds = d // LANE
    for s in range(ds):
        ref[pl.ds(s, rows, stride=ds), :] = mat[:, s * LANE:(s + 1) * LANE]


def _load_token_major(ref, rows, ds, s):
    return ref[pl.ds(s, rows, stride=ds), :]


def _router_kernel(x_ref, g_ref, sh_ref, sc_ref, wrt_ref, rb_ref, h_ref, eid_ref, wt_ref, *, n_experts):
    h = _rms_mod(x_ref[...], g_ref[...], sh_ref[0], sc_ref[0])
    _store_token_major(h_ref, h)

    logits = lax.dot_general(wrt_ref[...], h, (((1,), (1,)), ((), ())), precision=HIGHEST, preferred_element_type=F32)
    prob = jax.nn.sigmoid(logits)
    biased = prob + rb_ref[...]
    epg = n_experts // N_GROUPS
    p_rows = [prob[e:e + 1, :] for e in range(n_experts)]
    b_rows = [biased[e:e + 1, :] for e in range(n_experts)]

    scores = [_top2_sum(*b_rows[gi * epg:(gi + 1) * epg]) for gi in range(N_GROUPS)]
    grp = jnp.zeros_like(scores[0], dtype=jnp.int32)
    best = scores[0]
    for gi in range(1, N_GROUPS):
        better = scores[gi] > best
        grp = jnp.where(better, gi, grp)
        best = jnp.where(better, scores[gi], best)

    in_b = [_pick(grp, [b_rows[gi * epg + j] for gi in range(N_GROUPS)]) for j in range(epg)]
    in_p = [_pick(grp, [p_rows[gi * epg + j] for gi in range(N_GROUPS)]) for j in range(epg)]

    loc0 = jnp.zeros_like(grp)
    top = in_b[0]
    for j in range(1, epg):
        better = in_b[j] > top
        loc0 = jnp.where(better, j, loc0)
        top = jnp.where(better, in_b[j], top)
    loc1 = jnp.full_like(grp, -1)
    second = jnp.full_like(top, NEG_INF)
    for j in range(epg):
        better = (loc0 != j) & ((loc1 < 0) | (in_b[j] > second))
        loc1 = jnp.where(better, j, loc1)
        second = jnp.where(better, in_b[j], second)

    w0 = _pick(loc0, in_p)
    w1 = _pick(loc1, in_p)
    denom = w0 + w1
    pad_i = jnp.zeros((eid_ref.shape[0] - TOP_K, grp.shape[1]), jnp.int32)
    eid_ref[...] = jnp.concatenate([grp * epg + loc0, grp * epg + loc1, pad_i], axis=0)
    wt_ref[...] = jnp.concatenate([w0 / denom, w1 / denom, pad_i.astype(F32)], axis=0)


def _router(x2, g, shift, scale, w_router_t, router_bias, seq):
    n, d = x2.shape
    e = w_router_t.shape[0]
    ds = d // LANE
    tm = min(256, seq)
    tiles_per_batch = seq // tm
    kern = functools.partial(_router_kernel, n_experts=e)
    return pl.pallas_call(
        kern,
        grid=(n // tm,),
        in_specs=[
            pl.BlockSpec((tm, d), lambda i: (i, 0)),
            pl.BlockSpec((1, d), lambda i: (0, 0)),
            pl.BlockSpec((1, 1, d), lambda i: (i // tiles_per_batch, 0, 0)),
            pl.BlockSpec((1, 1, d), lambda i: (i // tiles_per_batch, 0, 0)),
            pl.BlockSpec((e, d), lambda i: (0, 0)),
            pl.BlockSpec((e, 1), lambda i: (0, 0)),
        ],
        out_specs=[
            pl.BlockSpec((tm * ds, LANE), lambda i: (i, 0)),
            pl.BlockSpec((8, tm), lambda i: (0, i)),
            pl.BlockSpec((8, tm), lambda i: (0, i)),
        ],
        out_shape=[
            jax.ShapeDtypeStruct((n * ds, LANE), F32),
            jax.ShapeDtypeStruct((8, n), jnp.int32),
            jax.ShapeDtypeStruct((8, n), F32),
        ],
        compiler_params=_params("parallel"),
        name="moe_router",
    )(x2, g.reshape(1, d), shift, scale, w_router_t, router_bias.reshape(e, 1))


def _gather_rows(idx_of, src_hbm, dst, sem, count, ds):
    def issue(r, carry):
        src = pl.ds(pl.multiple_of(idx_of(r) * ds, ds), ds)
        pltpu.make_async_copy(src_hbm.at[src], dst.at[pl.ds(pl.multiple_of(r * ds, ds), ds)], sem).start()
        return carry

    lax.fori_loop(0, count, issue, 0, unroll=8)


def _wait_rows(src_hbm, dst, sem):
    pltpu.make_async_copy(src_hbm.at[pl.ds(0, dst.shape[0])], dst, sem).wait()


def _expert_kernel(row_tok_ref, blk_exp_ref, n_used_ref, h_hbm, w_in_ref, w_out_ref, y_ref, buf, xs_scr, sem, *, d_ff):
    i = pl.program_id(0)
    tm, d = xs_scr.shape
    ds = d // LANE
    n_used = n_used_ref[0]
    slot = lax.bitwise_and(i, 1)

    def gather(blk, into):
        _gather_rows(lambda r: row_tok_ref[blk * tm + r], h_hbm, buf.at[into], sem.at[into], tm, ds)

    @pl.when(i == 0)
    def _():
        gather(0, 0)

    @pl.when(i < n_used)
    def _():
        @pl.when(i + 1 < n_used)
        def _():
            gather(i + 1, 1 - slot)

        _wait_rows(h_hbm, buf.at[slot], sem.at[slot])
        for s in range(ds):
            xs_scr[:, s * LANE:(s + 1) * LANE] = _load_token_major(buf.at[slot], tm, ds, s).astype(BF16)
        a = jnp.dot(xs_scr[...], w_in_ref[0, 0], preferred_element_type=F32)
        gate, up = a[:, :d_ff], a[:, d_ff:]
        act = (gate * jax.nn.sigmoid(gate) * up).astype(BF16)
        _store_token_major(y_ref, jnp.dot(act, w_out_ref[0, 0], preferred_element_type=F32))

    @pl.when(i >= n_used)
    def _():
        y_ref[...] = jnp.zeros_like(y_ref)


def _experts(h_tm, row_tok, blk_exp, n_used, w_in, w_out, layer):
    _, e, d, f2 = w_in.shape
    ds = d // LANE
    d_ff = f2 // 2
    n_rows = row_tok.shape[0]
    tm = MOE_ROW_BLOCK
    n_blk = n_rows // tm
    kern = functools.partial(_expert_kernel, d_ff=d_ff)
    grid_spec = pltpu.PrefetchScalarGridSpec(
        num_scalar_prefetch=3,
        grid=(n_blk,),
        in_specs=[
            pl.BlockSpec(memory_space=pl.ANY),
            pl.BlockSpec((1, 1, d, f2), lambda i, rt, be, nu: (layer, be[i], 0, 0)),
            pl.BlockSpec((1, 1, d_ff, d), lambda i, rt, be, nu: (layer, be[i], 0, 0)),
        ],
        out_specs=pl.BlockSpec((tm * ds, LANE), lambda i, rt, be, nu: (i, 0)),
        scratch_shapes=[
            pltpu.VMEM((2, tm * ds, LANE), F32),
            pltpu.VMEM((tm, d), BF16),
            pltpu.SemaphoreType.DMA((2,)),
        ],
    )
    return pl.pallas_call(
        kern,
        grid_spec=grid_spec,
        out_shape=jax.ShapeDtypeStruct((n_rows * ds, LANE), F32),
        compiler_params=_params("arbitrary"),
        name="moe_experts",
    )(row_tok, blk_exp, n_used, h_tm, w_in, w_out)


def _combine_kernel(pos_ref, y_hbm, x_ref, wt_ref, gt_ref, o_ref, buf, sem):
    i = pl.program_id(0)
    tm, d = x_ref.shape
    ds = d // LANE
    slot = lax.bitwise_and(i, 1)

    def gather(tile, into):
        for k in range(TOP_K):
            _gather_rows(lambda r, k=k: pos_ref[TOP_K * (tile * tm + r) + k], y_hbm, buf.at[into, k],
                         sem.at[into, k], tm, ds)

    @pl.when(i == 0)
    def _():
        gather(0, 0)

    @pl.when(i + 1 < pl.num_programs(0))
    def _():
        gather(i + 1, 1 - slot)

    for k in range(TOP_K):
        _wait_rows(y_hbm, buf.at[slot, k], sem.at[slot, k])
    w0 = wt_ref[:, 0:1]
    w1 = wt_ref[:, 1:2]
    for s in range(ds):
        cols = slice(s * LANE, (s + 1) * LANE)
        moe = (_load_token_major(buf.at[slot, 0], tm, ds, s) * w0
               + _load_token_major(buf.at[slot, 1], tm, ds, s) * w1)
        o_ref[:, cols] = x_ref[:, cols] + gt_ref[0][:, cols] * moe


def _combine(y_tm, pos, x2, wts, gate, seq):
    n, d = x2.shape
    ds = d // LANE
    tm = min(256, seq)
    tiles_per_batch = seq // tm
    grid_spec = pltpu.PrefetchScalarGridSpec(
        num_scalar_prefetch=1,
        grid=(n // tm,),
        in_specs=[
            pl.BlockSpec(memory_space=pl.ANY),
            pl.BlockSpec((tm, d), lambda i, p: (i, 0)),
            pl.BlockSpec((tm, TOP_K), lambda i, p: (i, 0)),
            pl.BlockSpec((1, 1, d), lambda i, p: (i // tiles_per_batch, 0, 0)),
        ],
        out_specs=pl.BlockSpec((tm, d), lambda i, p: (i, 0)),
        scratch_shapes=[
            pltpu.VMEM((2, TOP_K, tm * ds, LANE), F32),
            pltpu.SemaphoreType.DMA((2, TOP_K)),
        ],
    )
    return pl.pallas_call(
        _combine_kernel,
        grid_spec=grid_spec,
        out_shape=jax.ShapeDtypeStruct((n, d), F32),
        compiler_params=_params("arbitrary"),
        name="moe_combine",
    )(pos, y_tm, x2, wts, gate)


def _moe_layer(x2, g, shift, scale, gate, w_router_t, router_bias, w_in, w_out, layer, seq):
    n, d = x2.shape
    e = w_router_t.shape[0]
    h_tm, eid8, wt8 = _router(x2, g, shift, scale, w_router_t, router_bias, seq)
    eid = eid8[:TOP_K].T
    wts = wt8[:TOP_K].T

    rb = MOE_ROW_BLOCK
    nk = n * TOP_K
    e_flat = eid.reshape(nk)
    one_hot = (e_flat[:, None] == jnp.arange(e, dtype=jnp.int32)[None, :]).astype(jnp.int32)
    running = jnp.cumsum(one_hot, axis=0)
    counts = running[-1]
    rank = jnp.sum(running * one_hot, axis=1) - 1
    padded = ((counts + rb - 1) // rb) * rb
    ends = jnp.cumsum(padded)
    pos = ((ends - padded)[e_flat] + rank).astype(jnp.int32)
    n_rows = ((nk + e * (rb - 1) + rb - 1) // rb) * rb
    n_blk = n_rows // rb
    tok = jnp.arange(nk, dtype=jnp.int32) // TOP_K
    row_tok = jnp.zeros((n_rows,), jnp.int32).at[pos].set(tok)
    blk_exp = jnp.searchsorted(ends, jnp.arange(n_blk, dtype=jnp.int32) * rb, side="right")
    blk_exp = jnp.minimum(blk_exp, e - 1).astype(jnp.int32)
    n_used = (ends[-1] // rb).astype(jnp.int32).reshape(1)

    y_tm = _experts(h_tm, row_tok, blk_exp, n_used, w_in, w_out, layer)
    return _combine(y_tm, pos, x2, wts, gate, seq)


def kernel(x, c, g_attn, g_ffn, w_mod, b_mod, w_qkv_a, g_q_a, g_k_a, w_o_a, g_kv, w_mod_kv, b_mod_kv, w_kvf, b_f,
           g_k_b, w_q_b, g_q_b, w_o_b, w_router, router_bias, w_in, w_out):
    b, s, d = x.shape
    n = b * s
    depth = g_attn.shape[0]
    n_a = w_qkv_a.shape[0]
    n_heads = b_f.shape[0]
    hd = g_q_a.shape[-1]
    w = n_heads * hd
    assert hd == LANE and s % MOBA_BLOCK == 0 and d % LANE == 0

    c_pad = jnp.pad(c, ((0, 8 - b), (0, 0)))
    mod = _mod_call(c_pad, w_mod, b_mod)[:, :b]
    mod_kv = _mod_call(c_pad, w_mod_kv[None], b_mod_kv[None])[0, :b]
    slopes = jnp.exp2(-8.0 * jnp.arange(1, n_heads + 1, dtype=F32) / n_heads)
    slope_pieces = jnp.stack(_bf16_pieces(slopes * LOG2E), axis=1).reshape(3 * n_heads)
    q_scale = LOG2E * hd ** -0.5
    w_router_t = w_router.T
    ones_hd = jnp.ones((hd,), F32)
    w_in_bf = w_in.astype(BF16)
    w_out_bf = w_out.astype(BF16)

    x2 = x.reshape(n, d)
    kv = cum = None
    for l in range(depth):
        sh_a, sc_a, gt_a, sh_m, sc_m, gt_m = (mod[l][:, k * d:(k + 1) * d].reshape(b, 1, d) for k in range(6))
        if l < n_a:
            gains = jnp.stack([g_q_a[l] * q_scale, g_k_a[l], ones_hd]).reshape(3, 1, hd)
            qkv = _norm_proj(x2, g_attn[l], sh_a, sc_a, w_qkv_a[l].astype(BF16), gains, 2, s)
            o = _moba(qkv.reshape(b, s, 3 * w), slope_pieces, n_heads)
            w_o = w_o_a[l]
        else:
            j = l - n_a
            gains = (g_q_b[j] * q_scale).reshape(1, 1, hd)
            q = _norm_proj(x2, g_attn[l], sh_a, sc_a, w_q_b[j].astype(BF16), gains, 1, s)
            o = _fox(q.reshape(b, s, w), kv, cum, n_heads)
            w_o = w_o_b[j]
        x2 = _out_proj(o.reshape(n, w), w_o.astype(BF16), x2, gt_a, s)
        x2 = _moe_layer(x2, g_ffn[l], sh_m, sc_m, gt_m, w_router_t, router_bias, w_in_bf, w_out_bf, l, s)
        if l == n_a - 1:
            sh_kv = mod_kv[:, :d].reshape(b, 1, d)
            sc_kv = mod_kv[:, d:].reshape(b, 1, d)
            gains = jnp.stack([g_k_b, ones_hd]).reshape(2, 1, hd)
            kv = _norm_proj(x2, g_kv, sh_kv, sc_kv, w_kvf[:, :2 * w].astype(BF16), gains, 1, s).reshape(b, s, 2 * w)
            w_f = jnp.pad(w_kvf[:, 2 * w:], ((0, 0), (0, LANE - n_heads)))
            b_f_pad = jnp.pad(b_f, (0, LANE - n_heads)).reshape(1, LANE)
            cum3 = _decay_cumsum(x2, g_kv, sh_kv, sc_kv, w_f, b_f_pad, b, s)
            cum = cum3[:, :, :n_heads].transpose(0, 2, 1)
    return x2.reshape(b, s, d)
```

```python
import functools
import math

import jax
import jax.numpy as jnp
from jax import lax
from jax.experimental import pallas as pl
from jax.experimental.pallas import tpu as pltpu

MOBA_BLOCK = 256
MOBA_TOPK = 3
N_GROUPS = 4
TOP_K = 2
EPS = 1e-6

LANE = 128
V7X_VMEM_BYTES = 64 * 1024 * 1024
VMEM_LIMIT = V7X_VMEM_BYTES - 8 * 1024 * 1024

MOE_ROW_BLOCK = 256
ATTN_BLOCK = 256
ATTN_HEADS = 2
HIGHEST = lax.Precision.HIGHEST
BF16 = jnp.bfloat16
F32 = jnp.float32
NEG_INF = float("-inf")
MASKED = -1e30
LOG2E = math.log2(math.e)


def _params(*semantics):
    return pltpu.CompilerParams(dimension_semantics=semantics, vmem_limit_bytes=VMEM_LIMIT)


def _rms_mod(x, g, shift, scale):
    ms = jnp.mean(x * x, axis=-1, keepdims=True)
    y = x * lax.rsqrt(ms + EPS) * g
    return y * (1.0 + scale) + shift


def _bf16_pieces(x):
    hi = x.astype(BF16).astype(F32)
    mid = (x - hi).astype(BF16).astype(F32)
    lo = (x - hi - mid).astype(BF16).astype(F32)
    return hi, mid, lo


def _mod_kernel(c_ref, w_ref, b_ref, o_ref):
    c = c_ref[...]
    cond = c * jax.nn.sigmoid(c)
    o_ref[0] = jnp.dot(cond, w_ref[0], precision=HIGHEST, preferred_element_type=F32) + b_ref[0]


def _mod_call(c_pad, w, b):
    n_l, d, m = w.shape
    rows = c_pad.shape[0]
    tn = min(1024, m)
    return pl.pallas_call(
        _mod_kernel,
        grid=(n_l, m // tn),
        in_specs=[
            pl.BlockSpec((rows, d), lambda l, j: (0, 0)),
            pl.BlockSpec((1, d, tn), lambda l, j: (l, 0, j)),
            pl.BlockSpec((1, 1, tn), lambda l, j: (l, 0, j)),
        ],
        out_specs=pl.BlockSpec((1, rows, tn), lambda l, j: (l, 0, j)),
        out_shape=jax.ShapeDtypeStruct((n_l, rows, m), F32),
        compiler_params=_params("parallel", "parallel"),
        name="adaln_mod",
    )(c_pad, w, b.reshape(n_l, 1, m))


def _norm_proj_kernel(x_ref, g_ref, sh_ref, sc_ref, w_ref, hg_ref, o_ref, h_scr, *, n_norm_tiles, n_col_tiles):
    j = pl.program_id(1)

    @pl.when(j == 0)
    def _():
        h_scr[...] = _rms_mod(x_ref[...], g_ref[...], sh_ref[0], sc_ref[0]).astype(BF16)

    y = jnp.dot(h_scr[...], w_ref[...], preferred_element_type=F32)

    def write_head_normed():
        hg = hg_ref[0]
        for hh in range(y.shape[1] // LANE):
            yh = y[:, hh * LANE:(hh + 1) * LANE]
            ms = jnp.mean(yh * yh, axis=-1, keepdims=True)
            o_ref[:, hh * LANE:(hh + 1) * LANE] = (yh * lax.rsqrt(ms + EPS) * hg).astype(o_ref.dtype)

    def write_plain():
        o_ref[...] = y.astype(o_ref.dtype)

    if n_norm_tiles == n_col_tiles:
        write_head_normed()
    else:
        pl.when(j < n_norm_tiles)(write_head_normed)
        pl.when(j >= n_norm_tiles)(write_plain)


def _norm_proj(x2, g, shift, scale, w, head_gain, n_norm_sections, seq):
    n, d = x2.shape
    m = w.shape[1]
    n_sections = head_gain.shape[0]
    sec_w = m // n_sections
    tm = min(512, seq)
    tn = min(512, sec_w)
    tiles_per_batch = seq // tm
    tiles_per_section = sec_w // tn
    n_col_tiles = m // tn
    kern = functools.partial(_norm_proj_kernel, n_norm_tiles=n_norm_sections * tiles_per_section,
                             n_col_tiles=n_col_tiles)
    return pl.pallas_call(
        kern,
        grid=(n // tm, n_col_tiles),
        in_specs=[
            pl.BlockSpec((tm, d), lambda i, j: (i, 0)),
            pl.BlockSpec((1, d), lambda i, j: (0, 0)),
            pl.BlockSpec((1, 1, d), lambda i, j: (i // tiles_per_batch, 0, 0)),
            pl.BlockSpec((1, 1, d), lambda i, j: (i // tiles_per_batch, 0, 0)),
            pl.BlockSpec((d, tn), lambda i, j: (0, j)),
            pl.BlockSpec((1, 1, LANE), lambda i, j: (j // tiles_per_section, 0, 0)),
        ],
        out_specs=pl.BlockSpec((tm, tn), lambda i, j: (i, j)),
        out_shape=jax.ShapeDtypeStruct((n, m), BF16),
        scratch_shapes=[pltpu.VMEM((tm, d), BF16)],
        compiler_params=_params("parallel", "arbitrary"),
        name="norm_proj",
    )(x2, g.reshape(1, d), shift, scale, w, head_gain)


_MAX_FIXED_SHIFT = 40.0


def _causal(rows, width):
    row = lax.broadcasted_iota(jnp.int32, (rows, width), 0)
    col = lax.broadcasted_iota(jnp.int32, (rows, width), 1)
    return row >= col - (width - rows)


def _flash_fixed(i, q_wides, kp_scr, vp_scr, s_scr, acc_scr):
    tq = ATTN_BLOCK
    half = 2 * tq
    trip_keys = 2 * half
    heads = range(len(q_wides))
    trips = lax.shift_right_logical(i, 2)

    def scores(hh, start, size):
        return lax.dot_general(q_wides[hh], kp_scr[hh, pl.ds(start, size), :], (((1,), (1,)), ((), ())),
                               preferred_element_type=F32)

    def weighted_values(hh, p, start, size):
        return jnp.dot(p, vp_scr[hh, pl.ds(start, size), :], preferred_element_type=F32)

    def last_blocks(n_blk):
        size = n_blk * tq
        start = pl.multiple_of((i - (n_blk - 1)) * tq, tq)
        for hh in heads:
            s_scr[hh] = scores(hh, 0, half)
        for hh in heads:
            s = jnp.where(_causal(tq, size), scores(hh, start, size), MASKED)
            acc_scr[hh] = weighted_values(hh, jnp.exp2(s).astype(BF16), start, size)

    lax.switch(lax.bitwise_and(i, 3), [functools.partial(last_blocks, n) for n in (1, 2, 3, 4)])

    def body(t, carry):
        base = pl.multiple_of(t * trip_keys, trip_keys)
        ahead = pl.multiple_of(jnp.minimum(t + 1, trips - 1) * trip_keys, trip_keys)
        second_s = [scores(hh, base + half, half) for hh in heads]
        first_p = [jnp.exp2(s_scr[hh]).astype(BF16) for hh in heads]
        for hh in heads:
            s_scr[hh] = scores(hh, ahead, half)
        for hh in heads:
            p = jnp.concatenate([first_p[hh], jnp.exp2(second_s[hh]).astype(BF16)], axis=1)
            acc_scr[hh] += weighted_values(hh, p, base, trip_keys)
        return carry

    lax.fori_loop(0, trips, body, 0)
    return [acc_scr[hh] for hh in heads]


def _flash_online(i, q_wides, kp_scr, vp_scr):
    tq = ATTN_BLOCK
    chunk = 2 * tq
    heads = range(len(q_wides))
    n_chunks = lax.shift_right_logical(i, 1)

    def scores(hh, start, size):
        return lax.dot_general(q_wides[hh], kp_scr[hh, pl.ds(start, size), :], (((1,), (1,)), ((), ())),
                               preferred_element_type=F32)

    def weighted_values(hh, p, start, size):
        return jnp.dot(p.astype(BF16), vp_scr[hh, pl.ds(start, size), :], preferred_element_type=F32)

    def chunk_logits(start):
        out = []
        for hh in heads:
            s = scores(hh, start, chunk)
            out.append((s, jnp.max(s, axis=-1, keepdims=True)))
        return tuple(out)

    def diagonal(start, size):
        ahead = chunk_logits(0)
        ms, accs = [], []
        for hh in heads:
            s = jnp.where(_causal(tq, size), scores(hh, start, size), MASKED)
            m = jnp.max(s, axis=-1, keepdims=True)
            ms.append(m)
            accs.append(weighted_values(hh, jnp.exp2(s - m), start, size))
        return tuple(ms), tuple(accs), ahead

    carry = lax.cond(lax.bitwise_and(i, 1) == 1,
                     lambda: diagonal(pl.multiple_of((i - 1) * tq, tq), chunk),
                     lambda: diagonal(pl.multiple_of(i * tq, tq), tq))

    def body(jj, carry):
        ms, accs, now = carry
        start = pl.multiple_of(jj * chunk, chunk)
        ahead = chunk_logits(pl.multiple_of(jnp.minimum(jj + 1, n_chunks - 1) * chunk, chunk))
        new_ms, new_accs = [], []
        for hh in heads:
            s, s_max = now[hh]
            m_new = jnp.maximum(ms[hh], s_max)
            p = jnp.exp2(s - m_new)
            new_ms.append(m_new)
            new_accs.append(jnp.exp2(ms[hh] - m_new) * accs[hh] + weighted_values(hh, p, start, chunk))
        return tuple(new_ms), tuple(new_accs), ahead

    _, accs, _ = lax.fori_loop(0, n_chunks, body, carry)
    return accs


def _head_cols(hh):
    return slice(hh * LANE, (hh + 1) * LANE)


def _widen_values(v_ref, vp_scr, n_blocks):
    tq = ATTN_BLOCK
    ones_col = jnp.where(lax.broadcasted_iota(jnp.int32, (tq, LANE), 1) == 0, 1.0, 0.0).astype(BF16)

    def fill(jb, carry):
        rows = pl.ds(pl.multiple_of(jb * tq, tq), tq)
        for hh in range(ATTN_HEADS):
            vp_scr[hh, rows, 0:LANE] = v_ref[0, rows, _head_cols(hh)]
            vp_scr[hh, rows, LANE:2 * LANE] = ones_col
        return carry

    lax.fori_loop(0, n_blocks, fill, 0)


def _attend(i, q_wides, kp_scr, vp_scr, extra_scr, o_ref):
    if extra_scr:
        accs = _flash_fixed(i, q_wides, kp_scr, vp_scr, *extra_scr)
    else:
        accs = _flash_online(i, q_wides, kp_scr, vp_scr)
    for hh, acc in enumerate(accs):
        o_ref[0, :, _head_cols(hh)] = (acc[:, :LANE] / acc[:, LANE:LANE + 1]).astype(o_ref.dtype)


def _attn_scratch(seq, fixed_shift):
    scratch = [
        pltpu.VMEM((ATTN_HEADS, seq, 2 * LANE), BF16),
        pltpu.VMEM((ATTN_HEADS, seq, 2 * LANE), BF16),
    ]
    if fixed_shift:
        scratch.append(pltpu.VMEM((ATTN_HEADS, ATTN_BLOCK, 2 * ATTN_BLOCK), F32))
        scratch.append(pltpu.VMEM((ATTN_HEADS, ATTN_BLOCK, 2 * LANE), F32))
    return scratch


def _by_shift(shift, call):
    return lax.cond(shift[0] <= _MAX_FIXED_SHIFT, functools.partial(call, True), functools.partial(call, False))


_MOBA_MAX_BLOCKS = 16


def _moba_kernel(shift_ref, slopes_ref, q_ref, k_ref, v_ref, o_ref, kmean_scr, kp_scr, vp_scr, *extra_scr, n_blocks):
    hp = pl.program_id(1)
    i = pl.program_id(2)
    mb = MOBA_BLOCK
    lane = lax.broadcasted_iota(jnp.int32, (mb, LANE), 1)
    pos = lax.broadcasted_iota(jnp.int32, (mb, LANE), 0).astype(F32)

    def slope_lanes(hh, base, sign, init):
        out = init
        for p in range(3):
            piece = slopes_ref[3 * (ATTN_HEADS * hp + hh) + p]
            out = jnp.where((lane == base + p) | (lane == base + 3 + p), sign * piece, out)
        return out

    @pl.when(i == 0)
    def _():
        kmean_scr[...] = jnp.zeros_like(kmean_scr)
        _widen_values(v_ref, vp_scr, n_blocks)

        def fill(jb, carry):
            rows = pl.ds(pl.multiple_of(jb * mb, mb), mb)
            shared = jnp.where(lane == jb, 1.0, 0.0)
            shared = jnp.where((lane >= 22) & (lane < 25), pos, shared)
            shared = jnp.where((lane >= 25) & (lane < 28), jnp.asarray(jb * mb, F32), shared)
            shared = jnp.where(lane == 28, 1.0, shared)
            for hh in range(ATTN_HEADS):
                kb = k_ref[0, rows, _head_cols(hh)]
                kmean_scr[hh, pl.ds(jb, 1), :] = jnp.mean(kb.astype(F32), axis=0, keepdims=True)
                kp_scr[hh, rows, 0:LANE] = kb
                kp_scr[hh, rows, LANE:2 * LANE] = slope_lanes(hh, 16, -1.0, shared).astype(BF16)
            return carry

        lax.fori_loop(0, n_blocks, fill, 0)

    q_wides = []
    blk = lax.broadcasted_iota(jnp.int32, (_MOBA_MAX_BLOCKS, mb), 0)
    for hh in range(ATTN_HEADS):
        q = q_ref[0, :, _head_cols(hh)]
        gate = lax.dot_general(kmean_scr[hh], q.astype(F32), (((1,), (1,)), ((), ())),
                               precision=HIGHEST, preferred_element_type=F32)
        avail = blk < i
        open_blk = blk == i
        for _ in range(MOBA_TOPK):
            g = jnp.where(avail, gate, NEG_INF)
            gmax = jnp.max(g, axis=0, keepdims=True)
            first = jnp.min(jnp.where(avail & (g == gmax), blk, _MOBA_MAX_BLOCKS), axis=0, keepdims=True)
            pick = blk == first
            open_blk = open_blk | pick
            avail = avail & jnp.logical_not(pick)
        mask_t = jnp.concatenate([jnp.where(open_blk, 0.0, MASKED),
                                  jnp.zeros((LANE - _MOBA_MAX_BLOCKS, mb), F32)], axis=0)

        aug = mask_t.T
        aug = jnp.where((lane >= 16) & (lane < 19), pos, aug)
        aug = jnp.where((lane >= 19) & (lane < 22), jnp.asarray(i * mb, F32), aug)
        aug = slope_lanes(hh, 22, 1.0, aug)
        aug = jnp.where(lane == 28, -shift_ref[0], aug)
        q_wides.append(jnp.concatenate([q, aug.astype(BF16)], axis=1))
    _attend(i, q_wides, kp_scr, vp_scr, extra_scr, o_ref)


def _moba(qkv, shift, slope_pieces, n_heads):
    b, s, w3 = qkv.shape
    w = w3 // 3
    hd = w // n_heads
    nb = s // MOBA_BLOCK
    assert nb <= _MOBA_MAX_BLOCKS and hd == LANE and MOBA_BLOCK == ATTN_BLOCK
    assert n_heads % ATTN_HEADS == 0 and s >= 2 * ATTN_BLOCK
    kern = functools.partial(_moba_kernel, n_blocks=nb)
    groups = n_heads // ATTN_HEADS
    gw = ATTN_HEADS * hd

    def call(fixed_shift):
        return pl.pallas_call(
            kern,
            grid=(b, groups, nb),
            in_specs=[
                pl.BlockSpec(memory_space=pltpu.SMEM),
                pl.BlockSpec(memory_space=pltpu.SMEM),
                pl.BlockSpec((1, MOBA_BLOCK, gw), lambda bi, hp, i: (bi, i, hp)),
                pl.BlockSpec((1, s, gw), lambda bi, hp, i: (bi, 0, groups + hp)),
                pl.BlockSpec((1, s, gw), lambda bi, hp, i: (bi, 0, 2 * groups + hp)),
            ],
            out_specs=pl.BlockSpec((1, MOBA_BLOCK, gw), lambda bi, hp, i: (bi, i, hp)),
            out_shape=jax.ShapeDtypeStruct((b, s, w), BF16),
            scratch_shapes=[pltpu.VMEM((ATTN_HEADS, _MOBA_MAX_BLOCKS, hd), F32)] + _attn_scratch(s, fixed_shift),
            compiler_params=_params("parallel", "parallel", "arbitrary"),
            name="moba_attention" if fixed_shift else "moba_attention_online",
        )(shift, slope_pieces, qkv, qkv, qkv)

    return _by_shift(shift, call)


def _fox_kernel(shift_ref, q_ref, k_ref, v_ref, cum_ref, o_ref, qa_scr, kp_scr, vp_scr, *extra_scr, n_blocks):
    i = pl.program_id(2)
    tq = ATTN_BLOCK
    lane = lax.broadcasted_iota(jnp.int32, (tq, LANE), 1)

    @pl.when(i == 0)
    def _():
        _widen_values(v_ref, vp_scr, n_blocks)

        def fill(jb, carry):
            rows = pl.ds(pl.multiple_of(jb * tq, tq), tq)
            for hh in range(ATTN_HEADS):
                q_side = jnp.where((lane >= 3) & (lane < 6), 1.0, jnp.where(lane == 6, -shift_ref[0], 0.0))
                k_side = jnp.where((lane < 3) | (lane == 6), 1.0, 0.0)
                for p, piece in enumerate(_bf16_pieces(cum_ref[0, hh, rows, :] * LOG2E)):
                    q_side = jnp.where(lane == p, piece, q_side)
                    k_side = jnp.where(lane == 3 + p, -piece, k_side)
                qa_scr[hh, rows, :] = q_side.astype(BF16)
                kp_scr[hh, rows, 0:LANE] = k_ref[0, rows, _head_cols(hh)]
                kp_scr[hh, rows, LANE:2 * LANE] = k_side.astype(BF16)
            return carry

        lax.fori_loop(0, n_blocks, fill, 0)

    rows = pl.ds(pl.multiple_of(i * tq, tq), tq)
    q_wides = [jnp.concatenate([q_ref[0, :, _head_cols(hh)], qa_scr[hh, rows, :]], axis=1)
               for hh in range(ATTN_HEADS)]
    _attend(i, q_wides, kp_scr, vp_scr, extra_scr, o_ref)


def _fox(q, kv, cum, shift, n_heads):
    b, s, w = q.shape
    hd = w // n_heads
    tq = ATTN_BLOCK
    nt = s // tq
    assert hd == LANE and s % tq == 0 and n_heads % ATTN_HEADS == 0 and s >= 2 * tq
    kern = functools.partial(_fox_kernel, n_blocks=nt)
    groups = n_heads // ATTN_HEADS
    gw = ATTN_HEADS * hd

    def call(fixed_shift):
        return pl.pallas_call(
            kern,
            grid=(b, groups, nt),
            in_specs=[
                pl.BlockSpec(memory_space=pltpu.SMEM),
                pl.BlockSpec((1, tq, gw), lambda bi, hp, i: (bi, i, hp)),
                pl.BlockSpec((1, s, gw), lambda bi, hp, i: (bi, 0, hp)),
                pl.BlockSpec((1, s, gw), lambda bi, hp, i: (bi, 0, groups + hp)),
                pl.BlockSpec((1, ATTN_HEADS, s, 1), lambda bi, hp, i: (bi, hp, 0, 0)),
            ],
            out_specs=pl.BlockSpec((1, tq, gw), lambda bi, hp, i: (bi, i, hp)),
            out_shape=jax.ShapeDtypeStruct((b, s, w), BF16),
            scratch_shapes=[pltpu.VMEM((ATTN_HEADS, s, LANE), BF16)] + _attn_scratch(s, fixed_shift),
            compiler_params=_params("parallel", "parallel", "arbitrary"),
            name="fox_attention" if fixed_shift else "fox_attention_online",
        )(shift, q, kv, kv, cum.reshape(b, n_heads, s, 1))

    return _by_shift(shift, call)


def _decay_kernel(x_ref, g_ref, sh_ref, sc_ref, w_ref, b_ref, o_ref, carry_scr):
    t = pl.program_id(1)

    @pl.when(t == 0)
    def _():
        carry_scr[...] = jnp.zeros_like(carry_scr)

    z = _rms_mod(x_ref[...], g_ref[...], sh_ref[0], sc_ref[0])
    f = jnp.dot(z, w_ref[...], precision=HIGHEST, preferred_element_type=F32) + b_ref[...]
    log_f = jnp.minimum(f, 0.0) - jnp.log1p(jnp.exp(-jnp.abs(f)))
    tm = f.shape[0]
    lower = (lax.broadcasted_iota(jnp.int32, (tm, tm), 0) >= lax.broadcasted_iota(jnp.int32, (tm, tm), 1))
    cum = jnp.dot(jnp.where(lower, 1.0, 0.0), log_f, precision=HIGHEST, preferred_element_type=F32) + carry_scr[...]
    o_ref[0] = cum
    carry_scr[...] = cum[tm - 1:tm, :]


def _decay_cumsum(x2, g, shift, scale, w_f, b_f, batch, seq):
    n, d = x2.shape
    tm = min(512, seq)
    tiles = seq // tm
    return pl.pallas_call(
        _decay_kernel,
        grid=(batch, tiles),
        in_specs=[
            pl.BlockSpec((tm, d), lambda bi, t: (bi * tiles + t, 0)),
            pl.BlockSpec((1, d), lambda bi, t: (0, 0)),
            pl.BlockSpec((1, 1, d), lambda bi, t: (bi, 0, 0)),
            pl.BlockSpec((1, 1, d), lambda bi, t: (bi, 0, 0)),
            pl.BlockSpec((d, LANE), lambda bi, t: (0, 0)),
            pl.BlockSpec((1, LANE), lambda bi, t: (0, 0)),
        ],
        out_specs=pl.BlockSpec((1, tm, LANE), lambda bi, t: (bi, t, 0)),
        out_shape=jax.ShapeDtypeStruct((batch, seq, LANE), F32),
        scratch_shapes=[pltpu.VMEM((1, LANE), F32)],
        compiler_params=_params("parallel", "arbitrary"),
        name="decay_cumsum",
    )(x2, g.reshape(1, d), shift, scale, w_f, b_f)


def _out_proj_kernel(o_ref, w_ref, x_ref, gt_ref, y_ref):
    y = jnp.dot(o_ref[...], w_ref[...], preferred_element_type=F32)
    y_ref[...] = x_ref[...] + gt_ref[0] * y


def _out_proj(o2, w, x2, gate, seq):
    n, wdim = o2.shape
    d = w.shape[1]
    tm = min(512, seq)
    tn = min(1024, d)
    tiles_per_batch = seq // tm
    return pl.pallas_call(
        _out_proj_kernel,
        grid=(n // tm, d // tn),
        in_specs=[
            pl.BlockSpec((tm, wdim), lambda i, j: (i, 0)),
            pl.BlockSpec((wdim, tn), lambda i, j: (0, j)),
            pl.BlockSpec((tm, tn), lambda i, j: (i, j)),
            pl.BlockSpec((1, 1, tn), lambda i, j: (i // tiles_per_batch, 0, j)),
        ],
        out_specs=pl.BlockSpec((tm, tn), lambda i, j: (i, j)),
        out_shape=jax.ShapeDtypeStruct((n, d), F32),
        compiler_params=_params("parallel", "parallel"),
        name="out_proj",
    )(o2, w, x2, gate)


def _top2_sum(a, b, c, d):
    hi1, lo1 = jnp.maximum(a, b), jnp.minimum(a, b)
    hi2, lo2 = jnp.maximum(c, d), jnp.minimum(c, d)
    return jnp.maximum(hi1, hi2) + jnp.maximum(jnp.minimum(hi1, hi2), jnp.maximum(lo1, lo2))


def _pick(idx, rows):
    out = rows[-1]
    for k in range(len(rows) - 2, -1, -1):
        out = jnp.where(idx == k, rows[k], out)
    return out


def _store_token_major(ref, mat):
    rows, d = mat.shape
    ds = d // LANE
    for s in range(ds):
        ref[pl.ds(s, rows, stride=ds), :] = mat[:, s * LANE:(s + 1) * LANE]


def _load_token_major(ref, rows, ds, s):
    return ref[pl.ds(s, rows, stride=ds), :]


def _router_kernel(x_ref, g_ref, sh_ref, sc_ref, wrt_ref, rb_ref, h_ref, eid_ref, wt_ref, *, n_experts):
    h = _rms_mod(x_ref[...], g_ref[...], sh_ref[0], sc_ref[0])
    _store_token_major(h_ref, h)

    logits = lax.dot_general(wrt_ref[...], h, (((1,), (1,)), ((), ())), precision=HIGHEST, preferred_element_type=F32)
    prob = jax.nn.sigmoid(logits)
    biased = prob + rb_ref[...]
    epg = n_experts // N_GROUPS
    p_rows = [prob[e:e + 1, :] for e in range(n_experts)]
    b_rows = [biased[e:e + 1, :] for e in range(n_experts)]

    scores = [_top2_sum(*b_rows[gi * epg:(gi + 1) * epg]) for gi in range(N_GROUPS)]
    grp = jnp.zeros_like(scores[0], dtype=jnp.int32)
    best = scores[0]
    for gi in range(1, N_GROUPS):
        better = scores[gi] > best
        grp = jnp.where(better, gi, grp)
        best = jnp.where(better, scores[gi], best)

    in_b = [_pick(grp, [b_rows[gi * epg + j] for gi in range(N_GROUPS)]) for j in range(epg)]
    in_p = [_pick(grp, [p_rows[gi * epg + j] for gi in range(N_GROUPS)]) for j in range(epg)]

    loc0 = jnp.zeros_like(grp)
    top = in_b[0]
    for j in range(1, epg):
        better = in_b[j] > top
        loc0 = jnp.where(better, j, loc0)
        top = jnp.where(better, in_b[j], top)
    loc1 = jnp.full_like(grp, -1)
    second = jnp.full_like(top, NEG_INF)
    for j in range(epg):
        better = (loc0 != j) & ((loc1 < 0) | (in_b[j] > second))
        loc1 = jnp.where(better, j, loc1)
        second = jnp.where(better, in_b[j], second)

    w0 = _pick(loc0, in_p)
    w1 = _pick(loc1, in_p)
    denom = w0 + w1
    pad_i = jnp.zeros((eid_ref.shape[0] - TOP_K, grp.shape[1]), jnp.int32)
    eid_ref[...] = jnp.concatenate([grp * epg + loc0, grp * epg + loc1, pad_i], axis=0)
    wt_ref[...] = jnp.concatenate([w0 / denom, w1 / denom, pad_i.astype(F32)], axis=0)


def _router(x2, g, shift, scale, w_router_t, router_bias, seq):
    n, d = x2.shape
    e = w_router_t.shape[0]
    ds = d // LANE
    tm = min(256, seq)
    tiles_per_batch = seq // tm
    kern = functools.partial(_router_kernel, n_experts=e)
    return pl.pallas_call(
        kern,
        grid=(n // tm,),
        in_specs=[
            pl.BlockSpec((tm, d), lambda i: (i, 0)),
            pl.BlockSpec((1, d), lambda i: (0, 0)),
            pl.BlockSpec((1, 1, d), lambda i: (i // tiles_per_batch, 0, 0)),
            pl.BlockSpec((1, 1, d), lambda i: (i // tiles_per_batch, 0, 0)),
            pl.BlockSpec((e, d), lambda i: (0, 0)),
            pl.BlockSpec((e, 1), lambda i: (0, 0)),
        ],
        out_specs=[
            pl.BlockSpec((tm * ds, LANE), lambda i: (i, 0)),
            pl.BlockSpec((8, tm), lambda i: (0, i)),
            pl.BlockSpec((8, tm), lambda i: (0, i)),
        ],
        out_shape=[
            jax.ShapeDtypeStruct((n * ds, LANE), F32),
            jax.ShapeDtypeStruct((8, n), jnp.int32),
            jax.ShapeDtypeStruct((8, n), F32),
        ],
        compiler_params=_params("parallel"),
        name="moe_router",
    )(x2, g.reshape(1, d), shift, scale, w_router_t, router_bias.reshape(e, 1))


def _gather_rows(idx_of, src_hbm, dst, sem, count, ds):
    def issue(r, carry):
        src = pl.ds(pl.multiple_of(idx_of(r) * ds, ds), ds)
        pltpu.make_async_copy(src_hbm.at[src], dst.at[pl.ds(pl.multiple_of(r * ds, ds), ds)], sem).start()
        return carry

    lax.fori_loop(0, count, issue, 0, unroll=8)


def _wait_rows(src_hbm, dst, sem):
    pltpu.make_async_copy(src_hbm.at[pl.ds(0, dst.shape[0])], dst, sem).wait()


def _expert_kernel(row_tok_ref, blk_exp_ref, n_used_ref, h_hbm, w_in_ref, w_out_ref, y_ref, buf, xs_scr, sem, *, d_ff):
    i = pl.program_id(0)
    tm, d = xs_scr.shape
    ds = d // LANE
    n_used = n_used_ref[0]
    slot = lax.bitwise_and(i, 1)

    def gather(blk, into):
        _gather_rows(lambda r: row_tok_ref[blk * tm + r], h_hbm, buf.at[into], sem.at[into], tm, ds)

    @pl.when(i == 0)
    def _():
        gather(0, 0)

    @pl.when(i < n_used)
    def _():
        @pl.when(i + 1 < n_used)
        def _():
            gather(i + 1, 1 - slot)

        _wait_rows(h_hbm, buf.at[slot], sem.at[slot])
        for s in range(ds):
            xs_scr[:, s * LANE:(s + 1) * LANE] = _load_token_major(buf.at[slot], tm, ds, s).astype(BF16)
        a = jnp.dot(xs_scr[...], w_in_ref[0, 0], preferred_element_type=F32)
        gate, up = a[:, :d_ff], a[:, d_ff:]
        act = (gate * jax.nn.sigmoid(gate) * up).astype(BF16)
        _store_token_major(y_ref, jnp.dot(act, w_out_ref[0, 0], preferred_element_type=F32))

    @pl.when(i >= n_used)
    def _():
        y_ref[...] = jnp.zeros_like(y_ref)


def _experts(h_tm, row_tok, blk_exp, n_used, w_in, w_out, layer):
    _, e, d, f2 = w_in.shape
    ds = d // LANE
    d_ff = f2 // 2
    n_rows = row_tok.shape[0]
    tm = MOE_ROW_BLOCK
    n_blk = n_rows // tm
    kern = functools.partial(_expert_kernel, d_ff=d_ff)
    grid_spec = pltpu.PrefetchScalarGridSpec(
        num_scalar_prefetch=3,
        grid=(n_blk,),
        in_specs=[
            pl.BlockSpec(memory_space=pl.ANY),
            pl.BlockSpec((1, 1, d, f2), lambda i, rt, be, nu: (layer, be[i], 0, 0)),
            pl.BlockSpec((1, 1, d_ff, d), lambda i, rt, be, nu: (layer, be[i], 0, 0)),
        ],
        out_specs=pl.BlockSpec((tm * ds, LANE), lambda i, rt, be, nu: (i, 0)),
        scratch_shapes=[
            pltpu.VMEM((2, tm * ds, LANE), F32),
            pltpu.VMEM((tm, d), BF16),
            pltpu.SemaphoreType.DMA((2,)),
        ],
    )
    return pl.pallas_call(
        kern,
        grid_spec=grid_spec,
        out_shape=jax.ShapeDtypeStruct((n_rows * ds, LANE), F32),
        compiler_params=_params("arbitrary"),
        name="moe_experts",
    )(row_tok, blk_exp, n_used, h_tm, w_in, w_out)


def _combine_kernel(pos_ref, y_hbm, x_ref, wt_ref, gt_ref, o_ref, buf, sem):
    i = pl.program_id(0)
    tm, d = x_ref.shape
    ds = d // LANE
    slot = lax.bitwise_and(i, 1)

    def gather(tile, into):
        for k in range(TOP_K):
            _gather_rows(lambda r, k=k: pos_ref[TOP_K * (tile * tm + r) + k], y_hbm, buf.at[into, k],
                         sem.at[into, k], tm, ds)

    @pl.when(i == 0)
    def _():
        gather(0, 0)

    @pl.when(i + 1 < pl.num_programs(0))
    def _():
        gather(i + 1, 1 - slot)

    for k in range(TOP_K):
        _wait_rows(y_hbm, buf.at[slot, k], sem.at[slot, k])
    w0 = wt_ref[:, 0:1]
    w1 = wt_ref[:, 1:2]
    for s in range(ds):
        cols = slice(s * LANE, (s + 1) * LANE)
        moe = (_load_token_major(buf.at[slot, 0], tm, ds, s) * w0
               + _load_token_major(buf.at[slot, 1], tm, ds, s) * w1)
        o_ref[:, cols] = x_ref[:, cols] + gt_ref[0][:, cols] * moe


def _combine(y_tm, pos, x2, wts, gate, seq):
    n, d = x2.shape
    ds = d // LANE
    tm = min(256, seq)
    tiles_per_batch = seq // tm
    grid_spec = pltpu.PrefetchScalarGridSpec(
        num_scalar_prefetch=1,
        grid=(n // tm,),
        in_specs=[
            pl.BlockSpec(memory_space=pl.ANY),
            pl.BlockSpec((tm, d), lambda i, p: (i, 0)),
            pl.BlockSpec((tm, TOP_K), lambda i, p: (i, 0)),
            pl.BlockSpec((1, 1, d), lambda i, p: (i // tiles_per_batch, 0, 0)),
        ],
        out_specs=pl.BlockSpec((tm, d), lambda i, p: (i, 0)),
        scratch_shapes=[
            pltpu.VMEM((2, TOP_K, tm * ds, LANE), F32),
            pltpu.SemaphoreType.DMA((2, TOP_K)),
        ],
    )
    return pl.pallas_call(
        _combine_kernel,
        grid_spec=grid_spec,
        out_shape=jax.ShapeDtypeStruct((n, d), F32),
        compiler_params=_params("arbitrary"),
        name="moe_combine",
    )(pos, y_tm, x2, wts, gate)


def _moe_layer(x2, g, shift, scale, gate, w_router_t, router_bias, w_in, w_out, layer, seq):
    n, d = x2.shape
    e = w_router_t.shape[0]
    h_tm, eid8, wt8 = _router(x2, g, shift, scale, w_router_t, router_bias, seq)
    eid = eid8[:TOP_K].T
    wts = wt8[:TOP_K].T

    rb = MOE_ROW_BLOCK
    nk = n * TOP_K
    e_flat = eid.reshape(nk)
    one_hot = (e_flat[:, None] == jnp.arange(e, dtype=jnp.int32)[None, :]).astype(jnp.int32)
    running = jnp.cumsum(one_hot, axis=0)
    counts = running[-1]
    rank = jnp.sum(running * one_hot, axis=1) - 1
    padded = ((counts + rb - 1) // rb) * rb
    ends = jnp.cumsum(padded)
    pos = ((ends - padded)[e_flat] + rank).astype(jnp.int32)
    n_rows = ((nk + e * (rb - 1) + rb - 1) // rb) * rb
    n_blk = n_rows // rb
    tok = jnp.arange(nk, dtype=jnp.int32) // TOP_K
    row_tok = jnp.zeros((n_rows,), jnp.int32).at[pos].set(tok)
    blk_start = jnp.arange(n_blk, dtype=jnp.int32) * rb
    blk_exp = jnp.sum((ends[None, :] <= blk_start[:, None]).astype(jnp.int32), axis=1)
    blk_exp = jnp.minimum(blk_exp, e - 1).astype(jnp.int32)
    n_used = (ends[-1] // rb).astype(jnp.int32).reshape(1)

    y_tm = _experts(h_tm, row_tok, blk_exp, n_used, w_in, w_out, layer)
    return _combine(y_tm, pos, x2, wts, gate, seq)


def kernel(x, c, g_attn, g_ffn, w_mod, b_mod, w_qkv_a, g_q_a, g_k_a, w_o_a, g_kv, w_mod_kv, b_mod_kv, w_kvf, b_f,
           g_k_b, w_q_b, g_q_b, w_o_b, w_router, router_bias, w_in, w_out):
    b, s, d = x.shape
    n = b * s
    depth = g_attn.shape[0]
    n_a = w_qkv_a.shape[0]
    n_heads = b_f.shape[0]
    hd = g_q_a.shape[-1]
    w = n_heads * hd
    assert hd == LANE and s % MOBA_BLOCK == 0 and d % LANE == 0

    c_pad = jnp.pad(c, ((0, 8 - b), (0, 0)))
    mod = _mod_call(c_pad, w_mod, b_mod)[:, :b]
    mod_kv = _mod_call(c_pad, w_mod_kv[None], b_mod_kv[None])[0, :b]
    slopes = jnp.exp2(-8.0 * jnp.arange(1, n_heads + 1, dtype=F32) / n_heads)
    slope_pieces = jnp.stack(_bf16_pieces(slopes * LOG2E), axis=1).reshape(3 * n_heads)
    q_scale = LOG2E * hd ** -0.5

    def logit_bound(g_q, g_k):
        bound = 1.02 * LOG2E * hd ** 0.5 * jnp.max(jnp.abs(g_q)) * jnp.max(jnp.abs(g_k))
        return bound.astype(F32).reshape(1)
    w_router_t = w_router.T
    ones_hd = jnp.ones((hd,), F32)
    w_in_bf = w_in.astype(BF16)
    w_out_bf = w_out.astype(BF16)

    x2 = x.reshape(n, d)
    kv = cum = None
    for l in range(depth):
        sh_a, sc_a, gt_a, sh_m, sc_m, gt_m = (mod[l][:, k * d:(k + 1) * d].reshape(b, 1, d) for k in range(6))
        if l < n_a:
            gains = jnp.stack([g_q_a[l] * q_scale, g_k_a[l], ones_hd]).reshape(3, 1, hd)
            qkv = _norm_proj(x2, g_attn[l], sh_a, sc_a, w_qkv_a[l].astype(BF16), gains, 2, s)
            o = _moba(qkv.reshape(b, s, 3 * w), logit_bound(g_q_a[l], g_k_a[l]), slope_pieces, n_heads)
            w_o = w_o_a[l]
        else:
            j = l - n_a
            gains = (g_q_b[j] * q_scale).reshape(1, 1, hd)
            q = _norm_proj(x2, g_attn[l], sh_a, sc_a, w_q_b[j].astype(BF16), gains, 1, s)
            o = _fox(q.reshape(b, s, w), kv, cum, logit_bound(g_q_b[j], g_k_b), n_heads)
            w_o = w_o_b[j]
        x2 = _out_proj(o.reshape(n, w), w_o.astype(BF16), x2, gt_a, s)
        x2 = _moe_layer(x2, g_ffn[l], sh_m, sc_m, gt_m, w_router_t, router_bias, w_in_bf, w_out_bf, l, s)
        if l == n_a - 1:
            sh_kv = mod_kv[:, :d].reshape(b, 1, d)
            sc_kv = mod_kv[:, d:].reshape(b, 1, d)
            gains = jnp.stack([g_k_b, ones_hd]).reshape(2, 1, hd)
            kv = _norm_proj(x2, g_kv, sh_kv, sc_kv, w_kvf[:, :2 * w].astype(BF16), gains, 1, s).reshape(b, s, 2 * w)
            w_f = jnp.pad(w_kvf[:, 2 * w:], ((0, 0), (0, LANE - n_heads)))
            b_f_pad = jnp.pad(b_f, (0, LANE - n_heads)).reshape(1, LANE)
            cum3 = _decay_cumsum(x2, g_kv, sh_kv, sc_kv, w_f, b_f_pad, b, s)
            cum = cum3[:, :, :n_heads].transpose(0, 2, 1)
    return x2.reshape(b, s, d)
```

```python
import functools
import math

import jax
import jax.numpy as jnp
from jax import lax
from jax.experimental import pallas as pl
from jax.experimental.pallas import tpu as pltpu

MOBA_BLOCK = 256
MOBA_TOPK = 3
N_GROUPS = 4
TOP_K = 2
EPS = 1e-6

LANE = 128
MXU_COLS = 256
V7X_VMEM_BYTES = 64 * 1024 * 1024
VMEM_LIMIT = V7X_VMEM_BYTES - 8 * 1024 * 1024

MOE_ROW_BLOCK = 256
ATTN_BLOCK = 256
ATTN_HEADS = 2
HIGHEST = lax.Precision.HIGHEST
BF16 = jnp.bfloat16
F32 = jnp.float32
NEG_INF = float("-inf")
MASKED = -1e30
LOG2E = math.log2(math.e)


def _params(*semantics):
    return pltpu.CompilerParams(dimension_semantics=semantics, vmem_limit_bytes=VMEM_LIMIT)


def _rms_mod(x, g, shift, scale):
    ms = jnp.mean(x * x, axis=-1, keepdims=True)
    y = x * lax.rsqrt(ms + EPS) * g
    return y * (1.0 + scale) + shift


def _bf16_pieces(x):
    hi = x.astype(BF16).astype(F32)
    mid = (x - hi).astype(BF16).astype(F32)
    lo = (x - hi - mid).astype(BF16).astype(F32)
    return hi, mid, lo


def _mod_kernel(c_ref, w_ref, b_ref, o_ref):
    c = c_ref[...]
    cond = c * jax.nn.sigmoid(c)
    o_ref[0] = jnp.dot(cond, w_ref[0], precision=HIGHEST, preferred_element_type=F32) + b_ref[0]


def _mod_call(c_pad, w, b):
    n_l, d, m = w.shape
    rows = c_pad.shape[0]
    tn = min(1024, m)
    return pl.pallas_call(
        _mod_kernel,
        grid=(n_l, m // tn),
        in_specs=[
            pl.BlockSpec((rows, d), lambda l, j: (0, 0)),
            pl.BlockSpec((1, d, tn), lambda l, j: (l, 0, j)),
            pl.BlockSpec((1, 1, tn), lambda l, j: (l, 0, j)),
        ],
        out_specs=pl.BlockSpec((1, rows, tn), lambda l, j: (l, 0, j)),
        out_shape=jax.ShapeDtypeStruct((n_l, rows, m), F32),
        compiler_params=_params("parallel", "parallel"),
        name="adaln_mod",
    )(c_pad, w, b.reshape(n_l, 1, m))


def _norm_proj_kernel(x_ref, g_ref, sh_ref, sc_ref, w_ref, hg_ref, o_ref, h_scr, *, n_norm_tiles, n_col_tiles):
    j = pl.program_id(1)

    @pl.when(j == 0)
    def _():
        h_scr[...] = _rms_mod(x_ref[...], g_ref[...], sh_ref[0], sc_ref[0]).astype(BF16)

    def project(head_normed):
        hg = hg_ref[0]
        for c in range(0, o_ref.shape[1], MXU_COLS):
            y = jnp.dot(h_scr[...], w_ref[:, c:c + MXU_COLS], preferred_element_type=F32)
            if not head_normed:
                o_ref[:, c:c + MXU_COLS] = y.astype(o_ref.dtype)
                continue
            for hh in range(0, MXU_COLS, LANE):
                yh = y[:, hh:hh + LANE]
                ms = jnp.mean(yh * yh, axis=-1, keepdims=True)
                o_ref[:, c + hh:c + hh + LANE] = (yh * lax.rsqrt(ms + EPS) * hg).astype(o_ref.dtype)

    if n_norm_tiles == n_col_tiles:
        project(True)
    else:
        pl.when(j < n_norm_tiles)(functools.partial(project, True))
        pl.when(j >= n_norm_tiles)(functools.partial(project, False))


def _norm_proj(x2, g, shift, scale, w, head_gain, n_norm_sections, seq):
    n, d = x2.shape
    m = w.shape[1]
    n_sections = head_gain.shape[0]
    sec_w = m // n_sections
    tm = min(512, seq)
    tn = min(1024, sec_w)
    assert tn % MXU_COLS == 0
    tiles_per_batch = seq // tm
    tiles_per_section = sec_w // tn
    n_col_tiles = m // tn
    kern = functools.partial(_norm_proj_kernel, n_norm_tiles=n_norm_sections * tiles_per_section,
                             n_col_tiles=n_col_tiles)
    return pl.pallas_call(
        kern,
        grid=(n // tm, n_col_tiles),
        in_specs=[
            pl.BlockSpec((tm, d), lambda i, j: (i, 0)),
            pl.BlockSpec((1, d), lambda i, j: (0, 0)),
            pl.BlockSpec((1, 1, d), lambda i, j: (i // tiles_per_batch, 0, 0)),
            pl.BlockSpec((1, 1, d), lambda i, j: (i // tiles_per_batch, 0, 0)),
            pl.BlockSpec((d, tn), lambda i, j: (0, j)),
            pl.BlockSpec((1, 1, LANE), lambda i, j: (j // tiles_per_section, 0, 0)),
        ],
        out_specs=pl.BlockSpec((tm, tn), lambda i, j: (i, j)),
        out_shape=jax.ShapeDtypeStruct((n, m), BF16),
        scratch_shapes=[pltpu.VMEM((tm, d), BF16)],
        compiler_params=_params("parallel", "arbitrary"),
        name="norm_proj",
    )(x2, g.reshape(1, d), shift, scale, w, head_gain)


_MAX_FIXED_SHIFT = 40.0


def _causal(rows, width):
    row = lax.broadcasted_iota(jnp.int32, (rows, width), 0)
    col = lax.broadcasted_iota(jnp.int32, (rows, width), 1)
    return row >= col - (width - rows)


def _flash_fixed(i, q_wides, kp_scr, vp_scr, s_scr, acc_scr):
    tq = ATTN_BLOCK
    half = 2 * tq
    trip_keys = 2 * half
    heads = range(len(q_wides))
    trips = lax.shift_right_logical(i, 2)

    def scores(hh, start, size):
        return lax.dot_general(q_wides[hh], kp_scr[hh, pl.ds(start, size), :], (((1,), (1,)), ((), ())),
                               preferred_element_type=F32)

    def weighted_values(hh, p, start, size):
        return jnp.dot(p, vp_scr[hh, pl.ds(start, size), :], preferred_element_type=F32)

    def last_blocks(n_blk):
        size = n_blk * tq
        start = pl.multiple_of((i - (n_blk - 1)) * tq, tq)
        for hh in heads:
            s_scr[hh] = scores(hh, 0, half)
        for hh in heads:
            s = jnp.where(_causal(tq, size), scores(hh, start, size), MASKED)
            acc_scr[hh] = weighted_values(hh, jnp.exp2(s).astype(BF16), start, size)

    lax.switch(lax.bitwise_and(i, 3), [functools.partial(last_blocks, n) for n in (1, 2, 3, 4)])

    def body(t, carry):
        base = pl.multiple_of(t * trip_keys, trip_keys)
        ahead = pl.multiple_of(jnp.minimum(t + 1, trips - 1) * trip_keys, trip_keys)
        second_s = [scores(hh, base + half, half) for hh in heads]
        first_p = [jnp.exp2(s_scr[hh]).astype(BF16) for hh in heads]
        for hh in heads:
            s_scr[hh] = scores(hh, ahead, half)
        for hh in heads:
            p = jnp.concatenate([first_p[hh], jnp.exp2(second_s[hh]).astype(BF16)], axis=1)
            acc_scr[hh] += weighted_values(hh, p, base, trip_keys)
        return carry

    lax.fori_loop(0, trips, body, 0)
    return [acc_scr[hh] for hh in heads]


def _flash_online(i, q_wides, kp_scr, vp_scr):
    tq = ATTN_BLOCK
    chunk = 2 * tq
    heads = range(len(q_wides))
    n_chunks = lax.shift_right_logical(i, 1)

    def scores(hh, start, size):
        return lax.dot_general(q_wides[hh], kp_scr[hh, pl.ds(start, size), :], (((1,), (1,)), ((), ())),
                               preferred_element_type=F32)

    def weighted_values(hh, p, start, size):
        return jnp.dot(p.astype(BF16), vp_scr[hh, pl.ds(start, size), :], preferred_element_type=F32)

    def chunk_logits(start):
        out = []
        for hh in heads:
            s = scores(hh, start, chunk)
            out.append((s, jnp.max(s, axis=-1, keepdims=True)))
        return tuple(out)

    def diagonal(start, size):
        ahead = chunk_logits(0)
        ms, accs = [], []
        for hh in heads:
            s = jnp.where(_causal(tq, size), scores(hh, start, size), MASKED)
            m = jnp.max(s, axis=-1, keepdims=True)
            ms.append(m)
            accs.append(weighted_values(hh, jnp.exp2(s - m), start, size))
        return tuple(ms), tuple(accs), ahead

    carry = lax.cond(lax.bitwise_and(i, 1) == 1,
                     lambda: diagonal(pl.multiple_of((i - 1) * tq, tq), chunk),
                     lambda: diagonal(pl.multiple_of(i * tq, tq), tq))

    def body(jj, carry):
        ms, accs, now = carry
        start = pl.multiple_of(jj * chunk, chunk)
        ahead = chunk_logits(pl.multiple_of(jnp.minimum(jj + 1, n_chunks - 1) * chunk, chunk))
        new_ms, new_accs = [], []
        for hh in heads:
            s, s_max = now[hh]
            m_new = jnp.maximum(ms[hh], s_max)
            p = jnp.exp2(s - m_new)
            new_ms.append(m_new)
            new_accs.append(jnp.exp2(ms[hh] - m_new) * accs[hh] + weighted_values(hh, p, start, chunk))
        return tuple(new_ms), tuple(new_accs), ahead

    _, accs, _ = lax.fori_loop(0, n_chunks, body, carry)
    return accs


def _head_cols(hh):
    return slice(hh * LANE, (hh + 1) * LANE)


def _widen_values(v_ref, vp_scr, n_blocks):
    tq = ATTN_BLOCK
    ones_col = jnp.where(lax.broadcasted_iota(jnp.int32, (tq, LANE), 1) == 0, 1.0, 0.0).astype(BF16)

    def fill(jb, carry):
        rows = pl.ds(pl.multiple_of(jb * tq, tq), tq)
        for hh in range(ATTN_HEADS):
            vp_scr[hh, rows, 0:LANE] = v_ref[0, rows, _head_cols(hh)]
            vp_scr[hh, rows, LANE:2 * LANE] = ones_col
        return carry

    lax.fori_loop(0, n_blocks, fill, 0)


def _attend(i, q_wides, kp_scr, vp_scr, extra_scr, o_ref):
    if extra_scr:
        accs = _flash_fixed(i, q_wides, kp_scr, vp_scr, *extra_scr)
    else:
        accs = _flash_online(i, q_wides, kp_scr, vp_scr)
    for hh, acc in enumerate(accs):
        o_ref[0, :, _head_cols(hh)] = (acc[:, :LANE] / acc[:, LANE:LANE + 1]).astype(o_ref.dtype)


def _attn_scratch(seq, fixed_shift):
    scratch = [
        pltpu.VMEM((ATTN_HEADS, seq, 2 * LANE), BF16),
        pltpu.VMEM((ATTN_HEADS, seq, 2 * LANE), BF16),
    ]
    if fixed_shift:
        scratch.append(pltpu.VMEM((ATTN_HEADS, ATTN_BLOCK, 2 * ATTN_BLOCK), F32))
        scratch.append(pltpu.VMEM((ATTN_HEADS, ATTN_BLOCK, 2 * LANE), F32))
    return scratch


def _by_shift(shift, call):
    return lax.cond(shift[0] <= _MAX_FIXED_SHIFT, functools.partial(call, True), functools.partial(call, False))


_MOBA_MAX_BLOCKS = 16


def _moba_kernel(shift_ref, slopes_ref, q_ref, k_ref, v_ref, o_ref, kmean_scr, kp_scr, vp_scr, *extra_scr, n_blocks):
    hp = pl.program_id(1)
    i = pl.program_id(2)
    mb = MOBA_BLOCK
    lane = lax.broadcasted_iota(jnp.int32, (mb, LANE), 1)
    pos = lax.broadcasted_iota(jnp.int32, (mb, LANE), 0).astype(F32)

    def slope_lanes(hh, base, sign, init):
        out = init
        for p in range(3):
            piece = slopes_ref[3 * (ATTN_HEADS * hp + hh) + p]
            out = jnp.where((lane == base + p) | (lane == base + 3 + p), sign * piece, out)
        return out

    @pl.when(i == 0)
    def _():
        kmean_scr[...] = jnp.zeros_like(kmean_scr)
        _widen_values(v_ref, vp_scr, n_blocks)

        def fill(jb, carry):
            rows = pl.ds(pl.multiple_of(jb * mb, mb), mb)
            shared = jnp.where(lane == jb, 1.0, 0.0)
            shared = jnp.where((lane >= 22) & (lane < 25), pos, shared)
            shared = jnp.where((lane >= 25) & (lane < 28), jnp.asarray(jb * mb, F32), shared)
            shared = jnp.where(lane == 28, 1.0, shared)
            for hh in range(ATTN_HEADS):
                kb = k_ref[0, rows, _head_cols(hh)]
                kmean_scr[hh, pl.ds(jb, 1), :] = jnp.mean(kb.astype(F32), axis=0, keepdims=True)
                kp_scr[hh, rows, 0:LANE] = kb
                kp_scr[hh, rows, LANE:2 * LANE] = slope_lanes(hh, 16, -1.0, shared).astype(BF16)
            return carry

        lax.fori_loop(0, n_blocks, fill, 0)

    q_wides = []
    blk = lax.broadcasted_iota(jnp.int32, (_MOBA_MAX_BLOCKS, mb), 0)
    for hh in range(ATTN_HEADS):
        q = q_ref[0, :, _head_cols(hh)]
        gate = lax.dot_general(kmean_scr[hh], q.astype(F32), (((1,), (1,)), ((), ())),
                               precision=HIGHEST, preferred_element_type=F32)
        avail = blk < i
        open_blk = blk == i
        for _ in range(MOBA_TOPK):
            g = jnp.where(avail, gate, NEG_INF)
            gmax = jnp.max(g, axis=0, keepdims=True)
            first = jnp.min(jnp.where(avail & (g == gmax), blk, _MOBA_MAX_BLOCKS), axis=0, keepdims=True)
            pick = blk == first
            open_blk = open_blk | pick
            avail = avail & jnp.logical_not(pick)
        mask_t = jnp.concatenate([jnp.where(open_blk, 0.0, MASKED),
                                  jnp.zeros((LANE - _MOBA_MAX_BLOCKS, mb), F32)], axis=0)

        aug = mask_t.T
        aug = jnp.where((lane >= 16) & (lane < 19), pos, aug)
        aug = jnp.where((lane >= 19) & (lane < 22), jnp.asarray(i * mb, F32), aug)
        aug = slope_lanes(hh, 22, 1.0, aug)
        aug = jnp.where(lane == 28, -shift_ref[0], aug)
        q_wides.append(jnp.concatenate([q, aug.astype(BF16)], axis=1))
    _attend(i, q_wides, kp_scr, vp_scr, extra_scr, o_ref)


def _moba(qkv, shift, slope_pieces, n_heads):
    b, s, w3 = qkv.shape
    w = w3 // 3
    hd = w // n_heads
    nb = s // MOBA_BLOCK
    assert nb <= _MOBA_MAX_BLOCKS and hd == LANE and MOBA_BLOCK == ATTN_BLOCK
    assert n_heads % ATTN_HEADS == 0 and s >= 2 * ATTN_BLOCK
    kern = functools.partial(_moba_kernel, n_blocks=nb)
    groups = n_heads // ATTN_HEADS
    gw = ATTN_HEADS * hd

    def call(fixed_shift):
        return pl.pallas_call(
            kern,
            grid=(b, groups, nb),
            in_specs=[
                pl.BlockSpec(memory_space=pltpu.SMEM),
                pl.BlockSpec(memory_space=pltpu.SMEM),
                pl.BlockSpec((1, MOBA_BLOCK, gw), lambda bi, hp, i: (bi, i, hp)),
                pl.BlockSpec((1, s, gw), lambda bi, hp, i: (bi, 0, groups + hp)),
                pl.BlockSpec((1, s, gw), lambda bi, hp, i: (bi, 0, 2 * groups + hp)),
            ],
            out_specs=pl.BlockSpec((1, MOBA_BLOCK, gw), lambda bi, hp, i: (bi, i, hp)),
            out_shape=jax.ShapeDtypeStruct((b, s, w), BF16),
            scratch_shapes=[pltpu.VMEM((ATTN_HEADS, _MOBA_MAX_BLOCKS, hd), F32)] + _attn_scratch(s, fixed_shift),
            compiler_params=_params("parallel", "parallel", "arbitrary"),
            name="moba_attention" if fixed_shift else "moba_attention_online",
        )(shift, slope_pieces, qkv, qkv, qkv)

    return _by_shift(shift, call)


def _fox_kernel(shift_ref, q_ref, k_ref, v_ref, cum_ref, o_ref, qa_scr, kp_scr, vp_scr, *extra_scr, n_blocks):
    i = pl.program_id(2)
    tq = ATTN_BLOCK
    lane = lax.broadcasted_iota(jnp.int32, (tq, LANE), 1)

    @pl.when(i == 0)
    def _():
        _widen_values(v_ref, vp_scr, n_blocks)

        def fill(jb, carry):
            rows = pl.ds(pl.multiple_of(jb * tq, tq), tq)
            for hh in range(ATTN_HEADS):
                q_side = jnp.where((lane >= 3) & (lane < 6), 1.0, jnp.where(lane == 6, -shift_ref[0], 0.0))
                k_side = jnp.where((lane < 3) | (lane == 6), 1.0, 0.0)
                for p, piece in enumerate(_bf16_pieces(cum_ref[0, hh, rows, :] * LOG2E)):
                    q_side = jnp.where(lane == p, piece, q_side)
                    k_side = jnp.where(lane == 3 + p, -piece, k_side)
                qa_scr[hh, rows, :] = q_side.astype(BF16)
                kp_scr[hh, rows, 0:LANE] = k_ref[0, rows, _head_cols(hh)]
                kp_scr[hh, rows, LANE:2 * LANE] = k_side.astype(BF16)
            return carry

        lax.fori_loop(0, n_blocks, fill, 0)

    rows = pl.ds(pl.multiple_of(i * tq, tq), tq)
    q_wides = [jnp.concatenate([q_ref[0, :, _head_cols(hh)], qa_scr[hh, rows, :]], axis=1)
               for hh in range(ATTN_HEADS)]
    _attend(i, q_wides, kp_scr, vp_scr, extra_scr, o_ref)


def _fox(q, kv, cum, shift, n_heads):
    b, s, w = q.shape
    hd = w // n_heads
    tq = ATTN_BLOCK
    nt = s // tq
    assert hd == LANE and s % tq == 0 and n_heads % ATTN_HEADS == 0 and s >= 2 * tq
    kern = functools.partial(_fox_kernel, n_blocks=nt)
    groups = n_heads // ATTN_HEADS
    gw = ATTN_HEADS * hd

    def call(fixed_shift):
        return pl.pallas_call(
            kern,
            grid=(b, groups, nt),
            in_specs=[
                pl.BlockSpec(memory_space=pltpu.SMEM),
                pl.BlockSpec((1, tq, gw), lambda bi, hp, i: (bi, i, hp)),
                pl.BlockSpec((1, s, gw), lambda bi, hp, i: (bi, 0, hp)),
                pl.BlockSpec((1, s, gw), lambda bi, hp, i: (bi, 0, groups + hp)),
                pl.BlockSpec((1, ATTN_HEADS, s, 1), lambda bi, hp, i: (bi, hp, 0, 0)),
            ],
            out_specs=pl.BlockSpec((1, tq, gw), lambda bi, hp, i: (bi, i, hp)),
            out_shape=jax.ShapeDtypeStruct((b, s, w), BF16),
            scratch_shapes=[pltpu.VMEM((ATTN_HEADS, s, LANE), BF16)] + _attn_scratch(s, fixed_shift),
            compiler_params=_params("parallel", "parallel", "arbitrary"),
            name="fox_attention" if fixed_shift else "fox_attention_online",
        )(shift, q, kv, kv, cum.reshape(b, n_heads, s, 1))

    return _by_shift(shift, call)


def _decay_kernel(x_ref, g_ref, sh_ref, sc_ref, w_ref, b_ref, o_ref, carry_scr):
    t = pl.program_id(1)

    @pl.when(t == 0)
    def _():
        carry_scr[...] = jnp.zeros_like(carry_scr)

    z = _rms_mod(x_ref[...], g_ref[...], sh_ref[0], sc_ref[0])
    f = jnp.dot(z, w_ref[...], precision=HIGHEST, preferred_element_type=F32) + b_ref[...]
    log_f = jnp.minimum(f, 0.0) - jnp.log1p(jnp.exp(-jnp.abs(f)))
    tm = f.shape[0]
    lower = (lax.broadcasted_iota(jnp.int32, (tm, tm), 0) >= lax.broadcasted_iota(jnp.int32, (tm, tm), 1))
    cum = jnp.dot(jnp.where(lower, 1.0, 0.0), log_f, precision=HIGHEST, preferred_element_type=F32) + carry_scr[...]
    o_ref[0] = cum
    carry_scr[...] = cum[tm - 1:tm, :]


def _decay_cumsum(x2, g, shift, scale, w_f, b_f, batch, seq):
    n, d = x2.shape
    tm = min(512, seq)
    tiles = seq // tm
    return pl.pallas_call(
        _decay_kernel,
        grid=(batch, tiles),
        in_specs=[
            pl.BlockSpec((tm, d), lambda bi, t: (bi * tiles + t, 0)),
            pl.BlockSpec((1, d), lambda bi, t: (0, 0)),
            pl.BlockSpec((1, 1, d), lambda bi, t: (bi, 0, 0)),
            pl.BlockSpec((1, 1, d), lambda bi, t: (bi, 0, 0)),
            pl.BlockSpec((d, LANE), lambda bi, t: (0, 0)),
            pl.BlockSpec((1, LANE), lambda bi, t: (0, 0)),
        ],
        out_specs=pl.BlockSpec((1, tm, LANE), lambda bi, t: (bi, t, 0)),
        out_shape=jax.ShapeDtypeStruct((batch, seq, LANE), F32),
        scratch_shapes=[pltpu.VMEM((1, LANE), F32)],
        compiler_params=_params("parallel", "arbitrary"),
        name="decay_cumsum",
    )(x2, g.reshape(1, d), shift, scale, w_f, b_f)


def _out_proj_kernel(o_ref, w_ref, x_ref, gt_ref, y_ref):
    for c in range(0, y_ref.shape[1], MXU_COLS):
        cols = slice(c, c + MXU_COLS)
        y = jnp.dot(o_ref[...], w_ref[:, cols], preferred_element_type=F32)
        y_ref[:, cols] = x_ref[:, cols] + gt_ref[0][:, cols] * y


def _out_proj(o2, w, x2, gate, seq):
    n, wdim = o2.shape
    d = w.shape[1]
    tm = min(512, seq)
    tn = min(1024, d)
    tiles_per_batch = seq // tm
    return pl.pallas_call(
        _out_proj_kernel,
        grid=(n // tm, d // tn),
        in_specs=[
            pl.BlockSpec((tm, wdim), lambda i, j: (i, 0)),
            pl.BlockSpec((wdim, tn), lambda i, j: (0, j)),
            pl.BlockSpec((tm, tn), lambda i, j: (i, j)),
            pl.BlockSpec((1, 1, tn), lambda i, j: (i // tiles_per_batch, 0, j)),
        ],
        out_specs=pl.BlockSpec((tm, tn), lambda i, j: (i, j)),
        out_shape=jax.ShapeDtypeStruct((n, d), F32),
        compiler_params=_params("parallel", "parallel"),
        name="out_proj",
    )(o2, w, x2, gate)


def _top2_sum(a, b, c, d):
    hi1, lo1 = jnp.maximum(a, b), jnp.minimum(a, b)
    hi2, lo2 = jnp.maximum(c, d), jnp.minimum(c, d)
    return jnp.maximum(hi1, hi2) + jnp.maximum(jnp.minimum(hi1, hi2), jnp.maximum(lo1, lo2))


def _pick(idx, rows):
    out = rows[-1]
    for k in range(len(rows) - 2, -1, -1):
        out = jnp.where(idx == k, rows[k], out)
    return out


def _store_token_major(ref, mat):
    rows, d = mat.shape
    ds = d // LANE
    for s in range(ds):
        ref[pl.ds(s, rows, stride=ds), :] = mat[:, s * LANE:(s + 1) * LANE]


def _load_token_major(ref, rows, ds, s):
    return ref[pl.ds(s, rows, stride=ds), :]


def _router_kernel(x_ref, g_ref, sh_ref, sc_ref, wrt_ref, rb_ref, h_ref, eid_ref, wt_ref, *, n_experts):
    h = _rms_mod(x_ref[...], g_ref[...], sh_ref[0], sc_ref[0])
    _store_token_major(h_ref, h)

    logits = lax.dot_general(wrt_ref[...], h, (((1,), (1,)), ((), ())), precision=HIGHEST, preferred_element_type=F32)
    prob = jax.nn.sigmoid(logits)
    biased = prob + rb_ref[...]
    epg = n_experts // N_GROUPS
    p_rows = [prob[e:e + 1, :] for e in range(n_experts)]
    b_rows = [biased[e:e + 1, :] for e in range(n_experts)]

    scores = [_top2_sum(*b_rows[gi * epg:(gi + 1) * epg]) for gi in range(N_GROUPS)]
    grp = jnp.zeros_like(scores[0], dtype=jnp.int32)
    best = scores[0]
    for gi in range(1, N_GROUPS):
        better = scores[gi] > best
        grp = jnp.where(better, gi, grp)
        best = jnp.where(better, scores[gi], best)

    in_b = [_pick(grp, [b_rows[gi * epg + j] for gi in range(N_GROUPS)]) for j in range(epg)]
    in_p = [_pick(grp, [p_rows[gi * epg + j] for gi in range(N_GROUPS)]) for j in range(epg)]

    loc0 = jnp.zeros_like(grp)
    top = in_b[0]
    for j in range(1, epg):
        better = in_b[j] > top
        loc0 = jnp.where(better, j, loc0)
        top = jnp.where(better, in_b[j], top)
    loc1 = jnp.full_like(grp, -1)
    second = jnp.full_like(top, NEG_INF)
    for j in range(epg):
        better = (loc0 != j) & ((loc1 < 0) | (in_b[j] > second))
        loc1 = jnp.where(better, j, loc1)
        second = jnp.where(better, in_b[j], second)

    w0 = _pick(loc0, in_p)
    w1 = _pick(loc1, in_p)
    denom = w0 + w1
    pad_i = jnp.zeros((eid_ref.shape[0] - TOP_K, grp.shape[1]), jnp.int32)
    eid_ref[...] = jnp.concatenate([grp * epg + loc0, grp * epg + loc1, pad_i], axis=0)
    wt_ref[...] = jnp.concatenate([w0 / denom, w1 / denom, pad_i.astype(F32)], axis=0)


def _router(x2, g, shift, scale, w_router_t, router_bias, seq):
    n, d = x2.shape
    e = w_router_t.shape[0]
    ds = d // LANE
    tm = min(256, seq)
    tiles_per_batch = seq // tm
    kern = functools.partial(_router_kernel, n_experts=e)
    return pl.pallas_call(
        kern,
        grid=(n // tm,),
        in_specs=[
            pl.BlockSpec((tm, d), lambda i: (i, 0)),
            pl.BlockSpec((1, d), lambda i: (0, 0)),
            pl.BlockSpec((1, 1, d), lambda i: (i // tiles_per_batch, 0, 0)),
            pl.BlockSpec((1, 1, d), lambda i: (i // tiles_per_batch, 0, 0)),
            pl.BlockSpec((e, d), lambda i: (0, 0)),
            pl.BlockSpec((e, 1), lambda i: (0, 0)),
        ],
        out_specs=[
            pl.BlockSpec((tm * ds, LANE), lambda i: (i, 0)),
            pl.BlockSpec((8, tm), lambda i: (0, i)),
            pl.BlockSpec((8, tm), lambda i: (0, i)),
        ],
        out_shape=[
            jax.ShapeDtypeStruct((n * ds, LANE), F32),
            jax.ShapeDtypeStruct((8, n), jnp.int32),
            jax.ShapeDtypeStruct((8, n), F32),
        ],
        compiler_params=_params("parallel"),
        name="moe_router",
    )(x2, g.reshape(1, d), shift, scale, w_router_t, router_bias.reshape(e, 1))


def _gather_rows(idx_of, src_hbm, dst, sem, count, ds):
    def issue(r, carry):
        src = pl.ds(pl.multiple_of(idx_of(r) * ds, ds), ds)
        pltpu.make_async_copy(src_hbm.at[src], dst.at[pl.ds(pl.multiple_of(r * ds, ds), ds)], sem).start()
        return carry

    lax.fori_loop(0, count, issue, 0, unroll=8)


def _wait_rows(src_hbm, dst, sem):
    pltpu.make_async_copy(src_hbm.at[pl.ds(0, dst.shape[0])], dst, sem).wait()


def _expert_kernel(row_tok_ref, blk_exp_ref, n_used_ref, h_hbm, w_in_ref, w_out_ref, y_ref, buf, xs_scr, sem, *, d_ff):
    i = pl.program_id(0)
    tm, d = xs_scr.shape
    ds = d // LANE
    n_used = n_used_ref[0]
    slot = lax.bitwise_and(i, 1)

    def gather(blk, into):
        _gather_rows(lambda r: row_tok_ref[blk * tm + r], h_hbm, buf.at[into], sem.at[into], tm, ds)

    @pl.when(i == 0)
    def _():
        gather(0, 0)

    @pl.when(i < n_used)
    def _():
        _wait_rows(h_hbm, buf.at[slot], sem.at[slot])
        nxt = jnp.minimum(i + 1, n_used - 1)
        for r in range(tm):
            src = pl.ds(pl.multiple_of(row_tok_ref[nxt * tm + r] * ds, ds), ds)
            pltpu.make_async_copy(h_hbm.at[src], buf.at[1 - slot, pl.ds(r * ds, ds)], sem.at[1 - slot]).start()
        for s in range(ds):
            xs_scr[:, s * LANE:(s + 1) * LANE] = _load_token_major(buf.at[slot], tm, ds, s).astype(BF16)
        a = jnp.dot(xs_scr[...], w_in_ref[0, 0], preferred_element_type=F32)
        gate, up = a[:, :d_ff], a[:, d_ff:]
        act = (gate * jax.nn.sigmoid(gate) * up).astype(BF16)
        _store_token_major(y_ref, jnp.dot(act, w_out_ref[0, 0], preferred_element_type=F32))

        @pl.when(i + 1 >= n_used)
        def _():
            _wait_rows(h_hbm, buf.at[1 - slot], sem.at[1 - slot])

    @pl.when(i >= n_used)
    def _():
        y_ref[...] = jnp.zeros_like(y_ref)


def _experts(h_tm, row_tok, blk_exp, n_used, w_in, w_out, layer):
    _, e, d, f2 = w_in.shape
    ds = d // LANE
    d_ff = f2 // 2
    n_rows = row_tok.shape[0]
    tm = MOE_ROW_BLOCK
    n_blk = n_rows // tm
    kern = functools.partial(_expert_kernel, d_ff=d_ff)
    grid_spec = pltpu.PrefetchScalarGridSpec(
        num_scalar_prefetch=3,
        grid=(n_blk,),
        in_specs=[
            pl.BlockSpec(memory_space=pl.ANY),
            pl.BlockSpec((1, 1, d, f2), lambda i, rt, be, nu: (layer, be[i], 0, 0)),
            pl.BlockSpec((1, 1, d_ff, d), lambda i, rt, be, nu: (layer, be[i], 0, 0)),
        ],
        out_specs=pl.BlockSpec((tm * ds, LANE), lambda i, rt, be, nu: (i, 0)),
        scratch_shapes=[
            pltpu.VMEM((2, tm * ds, LANE), F32),
            pltpu.VMEM((tm, d), BF16),
            pltpu.SemaphoreType.DMA((2,)),
        ],
    )
    return pl.pallas_call(
        kern,
        grid_spec=grid_spec,
        out_shape=jax.ShapeDtypeStruct((n_rows * ds, LANE), F32),
        compiler_params=_params("arbitrary"),
        name="moe_experts",
    )(row_tok, blk_exp, n_used, h_tm, w_in, w_out)


def _combine_kernel(pos_ref, y_hbm, x_ref, wt_ref, gt_ref, o_ref, buf, sem):
    i = pl.program_id(0)
    tm, d = x_ref.shape
    ds = d // LANE
    slot = lax.bitwise_and(i, 1)

    def gather(tile, into):
        for k in range(TOP_K):
            _gather_rows(lambda r, k=k: pos_ref[TOP_K * (tile * tm + r) + k], y_hbm, buf.at[into, k],
                         sem.at[into, k], tm, ds)

    @pl.when(i == 0)
    def _():
        gather(0, 0)

    for k in range(TOP_K):
        _wait_rows(y_hbm, buf.at[slot, k], sem.at[slot, k])
    last = pl.num_programs(0) - 1
    nxt = jnp.minimum(i + 1, last)
    for r in range(tm):
        for k in range(TOP_K):
            src = pl.ds(pl.multiple_of(pos_ref[TOP_K * (nxt * tm + r) + k] * ds, ds), ds)
            pltpu.make_async_copy(y_hbm.at[src], buf.at[1 - slot, k, pl.ds(r * ds, ds)], sem.at[1 - slot, k]).start()
    w0 = wt_ref[:, 0:1]
    w1 = wt_ref[:, 1:2]
    for s in range(ds):
        cols = slice(s * LANE, (s + 1) * LANE)
        moe = (_load_token_major(buf.at[slot, 0], tm, ds, s) * w0
               + _load_token_major(buf.at[slot, 1], tm, ds, s) * w1)
        o_ref[:, cols] = x_ref[:, cols] + gt_ref[0][:, cols] * moe

    @pl.when(i == last)
    def _():
        for k in range(TOP_K):
            _wait_rows(y_hbm, buf.at[1 - slot, k], sem.at[1 - slot, k])


def _combine(y_tm, pos, x2, wts, gate, seq):
    n, d = x2.shape
    ds = d // LANE
    tm = min(256, seq)
    tiles_per_batch = seq // tm
    grid_spec = pltpu.PrefetchScalarGridSpec(
        num_scalar_prefetch=1,
        grid=(n // tm,),
        in_specs=[
            pl.BlockSpec(memory_space=pl.ANY),
            pl.BlockSpec((tm, d), lambda i, p: (i, 0)),
            pl.BlockSpec((tm, TOP_K), lambda i, p: (i, 0)),
            pl.BlockSpec((1, 1, d), lambda i, p: (i // tiles_per_batch, 0, 0)),
        ],
        out_specs=pl.BlockSpec((tm, d), lambda i, p: (i, 0)),
        scratch_shapes=[
            pltpu.VMEM((2, TOP_K, tm * ds, LANE), F32),
            pltpu.SemaphoreType.DMA((2, TOP_K)),
        ],
    )
    return pl.pallas_call(
        _combine_kernel,
        grid_spec=grid_spec,
        out_shape=jax.ShapeDtypeStruct((n, d), F32),
        compiler_params=_params("arbitrary"),
        name="moe_combine",
    )(pos, y_tm, x2, wts, gate)


def _moe_layer(x2, g, shift, scale, gate, w_router_t, router_bias, w_in, w_out, layer, seq):
    n, d = x2.shape
    e = w_router_t.shape[0]
    h_tm, eid8, wt8 = _router(x2, g, shift, scale, w_router_t, router_bias, seq)
    eid = eid8[:TOP_K].T
    wts = wt8[:TOP_K].T

    rb = MOE_ROW_BLOCK
    nk = n * TOP_K
    e_flat = eid.reshape(nk)
    one_hot = (e_flat[:, None] == jnp.arange(e, dtype=jnp.int32)[None, :]).astype(jnp.int32)
    running = jnp.cumsum(one_hot, axis=0)
    counts = running[-1]
    rank = jnp.sum(running * one_hot, axis=1) - 1
    padded = ((counts + rb - 1) // rb) * rb
    ends = jnp.cumsum(padded)
    pos = ((ends - padded)[e_flat] + rank).astype(jnp.int32)
    n_rows = ((nk + e * (rb - 1) + rb - 1) // rb) * rb
    n_blk = n_rows // rb
    tok = jnp.arange(nk, dtype=jnp.int32) // TOP_K
    row_tok = jnp.zeros((n_rows,), jnp.int32).at[pos].set(tok)
    blk_start = jnp.arange(n_blk, dtype=jnp.int32) * rb
    blk_exp = jnp.sum((ends[None, :] <= blk_start[:, None]).astype(jnp.int32), axis=1)
    blk_exp = jnp.minimum(blk_exp, e - 1).astype(jnp.int32)
    n_used = (ends[-1] // rb).astype(jnp.int32).reshape(1)

    y_tm = _experts(h_tm, row_tok, blk_exp, n_used, w_in, w_out, layer)
    return _combine(y_tm, pos, x2, wts, gate, seq)


def kernel(x, c, g_attn, g_ffn, w_mod, b_mod, w_qkv_a, g_q_a, g_k_a, w_o_a, g_kv, w_mod_kv, b_mod_kv, w_kvf, b_f,
           g_k_b, w_q_b, g_q_b, w_o_b, w_router, router_bias, w_in, w_out):
    b, s, d = x.shape
    n = b * s
    depth = g_attn.shape[0]
    n_a = w_qkv_a.shape[0]
    n_heads = b_f.shape[0]
    hd = g_q_a.shape[-1]
    w = n_heads * hd
    assert hd == LANE and s % MOBA_BLOCK == 0 and d % LANE == 0

    c_pad = jnp.pad(c, ((0, 8 - b), (0, 0)))
    mod = _mod_call(c_pad, w_mod, b_mod)[:, :b]
    mod_kv = _mod_call(c_pad, w_mod_kv[None], b_mod_kv[None])[0, :b]
    slopes = jnp.exp2(-8.0 * jnp.arange(1, n_heads + 1, dtype=F32) / n_heads)
    slope_pieces = jnp.stack(_bf16_pieces(slopes * LOG2E), axis=1).reshape(3 * n_heads)
    q_scale = LOG2E * hd ** -0.5

    def logit_bound(g_q, g_k):
        bound = 1.02 * LOG2E * hd ** 0.5 * jnp.max(jnp.abs(g_q)) * jnp.max(jnp.abs(g_k))
        return bound.astype(F32).reshape(1)
    w_router_t = w_router.T
    ones_hd = jnp.ones((hd,), F32)
    w_in_bf = w_in.astype(BF16)
    w_out_bf = w_out.astype(BF16)

    x2 = x.reshape(n, d)
    kv = cum = None
    for l in range(depth):
        sh_a, sc_a, gt_a, sh_m, sc_m, gt_m = (mod[l][:, k * d:(k + 1) * d].reshape(b, 1, d) for k in range(6))
        if l < n_a:
            gains = jnp.stack([g_q_a[l] * q_scale, g_k_a[l], ones_hd]).reshape(3, 1, hd)
            qkv = _norm_proj(x2, g_attn[l], sh_a, sc_a, w_qkv_a[l].astype(BF16), gains, 2, s)
            o = _moba(qkv.reshape(b, s, 3 * w), logit_bound(g_q_a[l], g_k_a[l]), slope_pieces, n_heads)
            w_o = w_o_a[l]
        else:
            j = l - n_a
            gains = (g_q_b[j] * q_scale).reshape(1, 1, hd)
            q = _norm_proj(x2, g_attn[l], sh_a, sc_a, w_q_b[j].astype(BF16), gains, 1, s)
            o = _fox(q.reshape(b, s, w), kv, cum, logit_bound(g_q_b[j], g_k_b), n_heads)
            w_o = w_o_b[j]
        x2 = _out_proj(o.reshape(n, w), w_o.astype(BF16), x2, gt_a, s)
        x2 = _moe_layer(x2, g_ffn[l], sh_m, sc_m, gt_m, w_router_t, router_bias, w_in_bf, w_out_bf, l, s)
        if l == n_a - 1:
            sh_kv = mod_kv[:, :d].reshape(b, 1, d)
            sc_kv = mod_kv[:, d:].reshape(b, 1, d)
            gains = jnp.stack([g_k_b, ones_hd]).reshape(2, 1, hd)
            kv = _norm_proj(x2, g_kv, sh_kv, sc_kv, w_kvf[:, :2 * w].astype(BF16), gains, 1, s).reshape(b, s, 2 * w)
            w_f = jnp.pad(w_kvf[:, 2 * w:], ((0, 0), (0, LANE - n_heads)))
            b_f_pad = jnp.pad(b_f, (0, LANE - n_heads)).reshape(1, LANE)
            cum3 = _decay_cumsum(x2, g_kv, sh_kv, sc_kv, w_f, b_f_pad, b, s)
            cum = cum3[:, :, :n_heads].transpose(0, 2, 1)
    return x2.reshape(b, s, d)
```

```python
import functools
import math

import jax
import jax.numpy as jnp
from jax import lax
from jax.experimental import pallas as pl
from jax.experimental.pallas import tpu as pltpu

MOBA_BLOCK = 256
MOBA_TOPK = 3
N_GROUPS = 4
TOP_K = 2
EPS = 1e-6

LANE = 128
MXU_COLS = 256
V7X_VMEM_BYTES = 64 * 1024 * 1024
VMEM_LIMIT = V7X_VMEM_BYTES - 8 * 1024 * 1024

MOE_ROW_BLOCK = 256
ATTN_BLOCK = 256
ATTN_HEADS = 2
HIGHEST = lax.Precision.HIGHEST
BF16 = jnp.bfloat16
F32 = jnp.float32
NEG_INF = float("-inf")
MASKED = -1e30
LOG2E = math.log2(math.e)


def _params(*semantics):
    return pltpu.CompilerParams(dimension_semantics=semantics, vmem_limit_bytes=VMEM_LIMIT)


def _rms_mod(x, g, shift, scale):
    ms = jnp.mean(x * x, axis=-1, keepdims=True)
    y = x * lax.rsqrt(ms + EPS) * g
    return y * (1.0 + scale) + shift


def _bf16_pieces(x):
    hi = x.astype(BF16).astype(F32)
    mid = (x - hi).astype(BF16).astype(F32)
    lo = (x - hi - mid).astype(BF16).astype(F32)
    return hi, mid, lo


def _mod_kernel(c_ref, w_ref, b_ref, o_ref):
    c = c_ref[...]
    cond = c * jax.nn.sigmoid(c)
    o_ref[0] = jnp.dot(cond, w_ref[0], precision=HIGHEST, preferred_element_type=F32) + b_ref[0]


def _mod_call(c_pad, w, b):
    n_l, d, m = w.shape
    rows = c_pad.shape[0]
    tn = min(1024, m)
    return pl.pallas_call(
        _mod_kernel,
        grid=(n_l, m // tn),
        in_specs=[
            pl.BlockSpec((rows, d), lambda l, j: (0, 0)),
            pl.BlockSpec((1, d, tn), lambda l, j: (l, 0, j)),
            pl.BlockSpec((1, 1, tn), lambda l, j: (l, 0, j)),
        ],
        out_specs=pl.BlockSpec((1, rows, tn), lambda l, j: (l, 0, j)),
        out_shape=jax.ShapeDtypeStruct((n_l, rows, m), F32),
        compiler_params=_params("parallel", "parallel"),
        name="adaln_mod",
    )(c_pad, w, b.reshape(n_l, 1, m))


def _norm_proj_kernel(x_ref, g_ref, sh_ref, sc_ref, w_ref, hg_ref, o_ref, h_scr, *, n_norm_tiles, n_col_tiles):
    j = pl.program_id(1)

    @pl.when(j == 0)
    def _():
        h_scr[...] = _rms_mod(x_ref[...], g_ref[...], sh_ref[0], sc_ref[0]).astype(BF16)

    def project(head_normed):
        hg = hg_ref[0]
        for c in range(0, o_ref.shape[1], MXU_COLS):
            y = jnp.dot(h_scr[...], w_ref[:, c:c + MXU_COLS], preferred_element_type=F32)
            if not head_normed:
                o_ref[:, c:c + MXU_COLS] = y.astype(o_ref.dtype)
                continue
            for hh in range(0, MXU_COLS, LANE):
                yh = y[:, hh:hh + LANE]
                ms = jnp.mean(yh * yh, axis=-1, keepdims=True)
                o_ref[:, c + hh:c + hh + LANE] = (yh * lax.rsqrt(ms + EPS) * hg).astype(o_ref.dtype)

    if n_norm_tiles == n_col_tiles:
        project(True)
    else:
        pl.when(j < n_norm_tiles)(functools.partial(project, True))
        pl.when(j >= n_norm_tiles)(functools.partial(project, False))


def _norm_proj(x2, g, shift, scale, w, head_gain, n_norm_sections, seq):
    n, d = x2.shape
    m = w.shape[1]
    n_sections = head_gain.shape[0]
    sec_w = m // n_sections
    tm = min(512, seq)
    tn = min(1024, sec_w)
    assert tn % MXU_COLS == 0
    tiles_per_batch = seq // tm
    tiles_per_section = sec_w // tn
    n_col_tiles = m // tn
    kern = functools.partial(_norm_proj_kernel, n_norm_tiles=n_norm_sections * tiles_per_section,
                             n_col_tiles=n_col_tiles)
    return pl.pallas_call(
        kern,
        grid=(n // tm, n_col_tiles),
        in_specs=[
            pl.BlockSpec((tm, d), lambda i, j: (i, 0)),
            pl.BlockSpec((1, d), lambda i, j: (0, 0)),
            pl.BlockSpec((1, 1, d), lambda i, j: (i // tiles_per_batch, 0, 0)),
            pl.BlockSpec((1, 1, d), lambda i, j: (i // tiles_per_batch, 0, 0)),
            pl.BlockSpec((d, tn), lambda i, j: (0, j)),
            pl.BlockSpec((1, 1, LANE), lambda i, j: (j // tiles_per_section, 0, 0)),
        ],
        out_specs=pl.BlockSpec((tm, tn), lambda i, j: (i, j)),
        out_shape=jax.ShapeDtypeStruct((n, m), BF16),
        scratch_shapes=[pltpu.VMEM((tm, d), BF16)],
        compiler_params=_params("parallel", "arbitrary"),
        name="norm_proj",
    )(x2, g.reshape(1, d), shift, scale, w, head_gain)


_MAX_FIXED_SHIFT = 40.0


def _causal(rows, width):
    row = lax.broadcasted_iota(jnp.int32, (rows, width), 0)
    col = lax.broadcasted_iota(jnp.int32, (rows, width), 1)
    return row >= col - (width - rows)


def _flash_fixed(i, q_wides, kp_scr, vp_scr, s_scr, acc_scr):
    tq = ATTN_BLOCK
    half = 2 * tq
    trip_keys = 2 * half
    heads = range(len(q_wides))
    trips = lax.shift_right_logical(i, 2)

    def scores(hh, start, size):
        return lax.dot_general(q_wides[hh], kp_scr[hh, pl.ds(start, size), :], (((1,), (1,)), ((), ())),
                               preferred_element_type=F32)

    def weighted_values(hh, p, start, size):
        return jnp.dot(p, vp_scr[hh, pl.ds(start, size), :], preferred_element_type=F32)

    def last_blocks(n_blk):
        size = n_blk * tq
        start = pl.multiple_of((i - (n_blk - 1)) * tq, tq)
        for hh in heads:
            s_scr[hh] = scores(hh, 0, half)
        for hh in heads:
            s = jnp.where(_causal(tq, size), scores(hh, start, size), MASKED)
            acc_scr[hh] = weighted_values(hh, jnp.exp2(s).astype(BF16), start, size)

    lax.switch(lax.bitwise_and(i, 3), [functools.partial(last_blocks, n) for n in (1, 2, 3, 4)])

    def body(t, carry):
        base = pl.multiple_of(t * trip_keys, trip_keys)
        ahead = pl.multiple_of(jnp.minimum(t + 1, trips - 1) * trip_keys, trip_keys)
        second_s = [scores(hh, base + half, half) for hh in heads]
        first_p = [jnp.exp2(s_scr[hh]).astype(BF16) for hh in heads]
        for hh in heads:
            s_scr[hh] = scores(hh, ahead, half)
        for hh in heads:
            p = jnp.concatenate([first_p[hh], jnp.exp2(second_s[hh]).astype(BF16)], axis=1)
            acc_scr[hh] += weighted_values(hh, p, base, trip_keys)
        return carry

    lax.fori_loop(0, trips, body, 0)
    return [acc_scr[hh] for hh in heads]


def _flash_online(i, q_wides, kp_scr, vp_scr):
    tq = ATTN_BLOCK
    chunk = 2 * tq
    heads = range(len(q_wides))
    n_chunks = lax.shift_right_logical(i, 1)

    def scores(hh, start, size):
        return lax.dot_general(q_wides[hh], kp_scr[hh, pl.ds(start, size), :], (((1,), (1,)), ((), ())),
                               preferred_element_type=F32)

    def weighted_values(hh, p, start, size):
        return jnp.dot(p.astype(BF16), vp_scr[hh, pl.ds(start, size), :], preferred_element_type=F32)

    def chunk_logits(start):
        out = []
        for hh in heads:
            s = scores(hh, start, chunk)
            out.append((s, jnp.max(s, axis=-1, keepdims=True)))
        return tuple(out)

    def diagonal(start, size):
        ahead = chunk_logits(0)
        ms, accs = [], []
        for hh in heads:
            s = jnp.where(_causal(tq, size), scores(hh, start, size), MASKED)
            m = jnp.max(s, axis=-1, keepdims=True)
            ms.append(m)
            accs.append(weighted_values(hh, jnp.exp2(s - m), start, size))
        return tuple(ms), tuple(accs), ahead

    carry = lax.cond(lax.bitwise_and(i, 1) == 1,
                     lambda: diagonal(pl.multiple_of((i - 1) * tq, tq), chunk),
                     lambda: diagonal(pl.multiple_of(i * tq, tq), tq))

    def body(jj, carry):
        ms, accs, now = carry
        start = pl.multiple_of(jj * chunk, chunk)
        ahead = chunk_logits(pl.multiple_of(jnp.minimum(jj + 1, n_chunks - 1) * chunk, chunk))
        new_ms, new_accs = [], []
        for hh in heads:
            s, s_max = now[hh]
            m_new = jnp.maximum(ms[hh], s_max)
            p = jnp.exp2(s - m_new)
            new_ms.append(m_new)
            new_accs.append(jnp.exp2(ms[hh] - m_new) * accs[hh] + weighted_values(hh, p, start, chunk))
        return tuple(new_ms), tuple(new_accs), ahead

    _, accs, _ = lax.fori_loop(0, n_chunks, body, carry)
    return accs


def _head_cols(hh):
    return slice(hh * LANE, (hh + 1) * LANE)


def _widen_values(v_ref, vp_scr, n_blocks):
    tq = ATTN_BLOCK
    ones_col = jnp.where(lax.broadcasted_iota(jnp.int32, (tq, LANE), 1) == 0, 1.0, 0.0).astype(BF16)

    def fill(jb, carry):
        rows = pl.ds(pl.multiple_of(jb * tq, tq), tq)
        for hh in range(ATTN_HEADS):
            vp_scr[hh, rows, 0:LANE] = v_ref[0, rows, _head_cols(hh)]
            vp_scr[hh, rows, LANE:2 * LANE] = ones_col
        return carry

    lax.fori_loop(0, n_blocks, fill, 0)


def _attend(i, q_wides, kp_scr, vp_scr, extra_scr, o_ref):
    if extra_scr:
        accs = _flash_fixed(i, q_wides, kp_scr, vp_scr, *extra_scr)
    else:
        accs = _flash_online(i, q_wides, kp_scr, vp_scr)
    for hh, acc in enumerate(accs):
        o_ref[0, :, _head_cols(hh)] = (acc[:, :LANE] / acc[:, LANE:LANE + 1]).astype(o_ref.dtype)


def _attn_scratch(seq, fixed_shift):
    scratch = [
        pltpu.VMEM((ATTN_HEADS, seq, 2 * LANE), BF16),
        pltpu.VMEM((ATTN_HEADS, seq, 2 * LANE), BF16),
    ]
    if fixed_shift:
        scratch.append(pltpu.VMEM((ATTN_HEADS, ATTN_BLOCK, 2 * ATTN_BLOCK), F32))
        scratch.append(pltpu.VMEM((ATTN_HEADS, ATTN_BLOCK, 2 * LANE), F32))
    return scratch


def _by_shift(shift, call):
    return lax.cond(shift[0] <= _MAX_FIXED_SHIFT, functools.partial(call, True), functools.partial(call, False))


_MOBA_MAX_BLOCKS = 16


def _moba_kernel(shift_ref, slopes_ref, q_ref, k_ref, v_ref, o_ref, kmean_scr, kp_scr, vp_scr, *extra_scr, n_blocks):
    hp = pl.program_id(1)
    i = pl.program_id(2)
    mb = MOBA_BLOCK
    lane = lax.broadcasted_iota(jnp.int32, (mb, LANE), 1)
    pos = lax.broadcasted_iota(jnp.int32, (mb, LANE), 0).astype(F32)

    def slope_lanes(hh, base, sign, init):
        out = init
        for p in range(3):
            piece = slopes_ref[3 * (ATTN_HEADS * hp + hh) + p]
            out = jnp.where((lane == base + p) | (lane == base + 3 + p), sign * piece, out)
        return out

    @pl.when(i == 0)
    def _():
        kmean_scr[...] = jnp.zeros_like(kmean_scr)
        _widen_values(v_ref, vp_scr, n_blocks)

        def fill(jb, carry):
            rows = pl.ds(pl.multiple_of(jb * mb, mb), mb)
            shared = jnp.where(lane == jb, 1.0, 0.0)
            shared = jnp.where((lane >= 22) & (lane < 25), pos, shared)
            shared = jnp.where((lane >= 25) & (lane < 28), jnp.asarray(jb * mb, F32), shared)
            shared = jnp.where(lane == 28, 1.0, shared)
            for hh in range(ATTN_HEADS):
                kb = k_ref[0, rows, _head_cols(hh)]
                kmean_scr[hh, pl.ds(jb, 1), :] = jnp.mean(kb.astype(F32), axis=0, keepdims=True)
                kp_scr[hh, rows, 0:LANE] = kb
                kp_scr[hh, rows, LANE:2 * LANE] = slope_lanes(hh, 16, -1.0, shared).astype(BF16)
            return carry

        lax.fori_loop(0, n_blocks, fill, 0)

    q_wides = []
    blk = lax.broadcasted_iota(jnp.int32, (_MOBA_MAX_BLOCKS, mb), 0)
    for hh in range(ATTN_HEADS):
        q = q_ref[0, :, _head_cols(hh)]
        gate = lax.dot_general(kmean_scr[hh], q.astype(F32), (((1,), (1,)), ((), ())),
                               precision=HIGHEST, preferred_element_type=F32)
        avail = blk < i
        open_blk = blk == i
        for _ in range(MOBA_TOPK):
            g = jnp.where(avail, gate, NEG_INF)
            gmax = jnp.max(g, axis=0, keepdims=True)
            first = jnp.min(jnp.where(avail & (g == gmax), blk, _MOBA_MAX_BLOCKS), axis=0, keepdims=True)
            pick = blk == first
            open_blk = open_blk | pick
            avail = avail & jnp.logical_not(pick)
        mask_t = jnp.concatenate([jnp.where(open_blk, 0.0, MASKED),
                                  jnp.zeros((LANE - _MOBA_MAX_BLOCKS, mb), F32)], axis=0)

        aug = mask_t.T
        aug = jnp.where((lane >= 16) & (lane < 19), pos, aug)
        aug = jnp.where((lane >= 19) & (lane < 22), jnp.asarray(i * mb, F32), aug)
        aug = slope_lanes(hh, 22, 1.0, aug)
        aug = jnp.where(lane == 28, -shift_ref[0], aug)
        q_wides.append(jnp.concatenate([q, aug.astype(BF16)], axis=1))
    _attend(i, q_wides, kp_scr, vp_scr, extra_scr, o_ref)


def _moba(qkv, shift, slope_pieces, n_heads):
    b, s, w3 = qkv.shape
    w = w3 // 3
    hd = w // n_heads
    nb = s // MOBA_BLOCK
    assert nb <= _MOBA_MAX_BLOCKS and hd == LANE and MOBA_BLOCK == ATTN_BLOCK
    assert n_heads % ATTN_HEADS == 0 and s >= 2 * ATTN_BLOCK
    kern = functools.partial(_moba_kernel, n_blocks=nb)
    groups = n_heads // ATTN_HEADS
    gw = ATTN_HEADS * hd

    def call(fixed_shift):
        return pl.pallas_call(
            kern,
            grid=(b, groups, nb),
            in_specs=[
                pl.BlockSpec(memory_space=pltpu.SMEM),
                pl.BlockSpec(memory_space=pltpu.SMEM),
                pl.BlockSpec((1, MOBA_BLOCK, gw), lambda bi, hp, i: (bi, i, hp)),
                pl.BlockSpec((1, s, gw), lambda bi, hp, i: (bi, 0, groups + hp)),
                pl.BlockSpec((1, s, gw), lambda bi, hp, i: (bi, 0, 2 * groups + hp)),
            ],
            out_specs=pl.BlockSpec((1, MOBA_BLOCK, gw), lambda bi, hp, i: (bi, i, hp)),
            out_shape=jax.ShapeDtypeStruct((b, s, w), BF16),
            scratch_shapes=[pltpu.VMEM((ATTN_HEADS, _MOBA_MAX_BLOCKS, hd), F32)] + _attn_scratch(s, fixed_shift),
            compiler_params=_params("parallel", "parallel", "arbitrary"),
            name="moba_attention" if fixed_shift else "moba_attention_online",
        )(shift, slope_pieces, qkv, qkv, qkv)

    return _by_shift(shift, call)


def _fox_kernel(shift_ref, q_ref, k_ref, v_ref, cum_ref, o_ref, qa_scr, kp_scr, vp_scr, *extra_scr, n_blocks):
    i = pl.program_id(2)
    tq = ATTN_BLOCK
    lane = lax.broadcasted_iota(jnp.int32, (tq, LANE), 1)

    @pl.when(i == 0)
    def _():
        _widen_values(v_ref, vp_scr, n_blocks)

        def fill(jb, carry):
            rows = pl.ds(pl.multiple_of(jb * tq, tq), tq)
            for hh in range(ATTN_HEADS):
                q_side = jnp.where((lane >= 3) & (lane < 6), 1.0, jnp.where(lane == 6, -shift_ref[0], 0.0))
                k_side = jnp.where((lane < 3) | (lane == 6), 1.0, 0.0)
                for p, piece in enumerate(_bf16_pieces(cum_ref[0, hh, rows, :] * LOG2E)):
                    q_side = jnp.where(lane == p, piece, q_side)
                    k_side = jnp.where(lane == 3 + p, -piece, k_side)
                qa_scr[hh, rows, :] = q_side.astype(BF16)
                kp_scr[hh, rows, 0:LANE] = k_ref[0, rows, _head_cols(hh)]
                kp_scr[hh, rows, LANE:2 * LANE] = k_side.astype(BF16)
            return carry

        lax.fori_loop(0, n_blocks, fill, 0)

    rows = pl.ds(pl.multiple_of(i * tq, tq), tq)
    q_wides = [jnp.concatenate([q_ref[0, :, _head_cols(hh)], qa_scr[hh, rows, :]], axis=1)
               for hh in range(ATTN_HEADS)]
    _attend(i, q_wides, kp_scr, vp_scr, extra_scr, o_ref)


def _fox(q, kv, cum, shift, n_heads):
    b, s, w = q.shape
    hd = w // n_heads
    tq = ATTN_BLOCK
    nt = s // tq
    assert hd == LANE and s % tq == 0 and n_heads % ATTN_HEADS == 0 and s >= 2 * tq
    kern = functools.partial(_fox_kernel, n_blocks=nt)
    groups = n_heads // ATTN_HEADS
    gw = ATTN_HEADS * hd

    def call(fixed_shift):
        return pl.pallas_call(
            kern,
            grid=(b, groups, nt),
            in_specs=[
                pl.BlockSpec(memory_space=pltpu.SMEM),
                pl.BlockSpec((1, tq, gw), lambda bi, hp, i: (bi, i, hp)),
                pl.BlockSpec((1, s, gw), lambda bi, hp, i: (bi, 0, hp)),
                pl.BlockSpec((1, s, gw), lambda bi, hp, i: (bi, 0, groups + hp)),
                pl.BlockSpec((1, ATTN_HEADS, s, 1), lambda bi, hp, i: (bi, hp, 0, 0)),
            ],
            out_specs=pl.BlockSpec((1, tq, gw), lambda bi, hp, i: (bi, i, hp)),
            out_shape=jax.ShapeDtypeStruct((b, s, w), BF16),
            scratch_shapes=[pltpu.VMEM((ATTN_HEADS, s, LANE), BF16)] + _attn_scratch(s, fixed_shift),
            compiler_params=_params("parallel", "parallel", "arbitrary"),
            name="fox_attention" if fixed_shift else "fox_attention_online",
        )(shift, q, kv, kv, cum.reshape(b, n_heads, s, 1))

    return _by_shift(shift, call)


def _decay_kernel(x_ref, g_ref, sh_ref, sc_ref, w_ref, b_ref, o_ref, carry_scr):
    t = pl.program_id(1)

    @pl.when(t == 0)
    def _():
        carry_scr[...] = jnp.zeros_like(carry_scr)

    z = _rms_mod(x_ref[...], g_ref[...], sh_ref[0], sc_ref[0])
    f = jnp.dot(z, w_ref[...], precision=HIGHEST, preferred_element_type=F32) + b_ref[...]
    log_f = jnp.minimum(f, 0.0) - jnp.log1p(jnp.exp(-jnp.abs(f)))
    tm = f.shape[0]
    lower = (lax.broadcasted_iota(jnp.int32, (tm, tm), 0) >= lax.broadcasted_iota(jnp.int32, (tm, tm), 1))
    cum = jnp.dot(jnp.where(lower, 1.0, 0.0), log_f, precision=HIGHEST, preferred_element_type=F32) + carry_scr[...]
    o_ref[0] = cum
    carry_scr[...] = cum[tm - 1:tm, :]


def _decay_cumsum(x2, g, shift, scale, w_f, b_f, batch, seq):
    n, d = x2.shape
    tm = min(512, seq)
    tiles = seq // tm
    return pl.pallas_call(
        _decay_kernel,
        grid=(batch, tiles),
        in_specs=[
            pl.BlockSpec((tm, d), lambda bi, t: (bi * tiles + t, 0)),
            pl.BlockSpec((1, d), lambda bi, t: (0, 0)),
            pl.BlockSpec((1, 1, d), lambda bi, t: (bi, 0, 0)),
            pl.BlockSpec((1, 1, d), lambda bi, t: (bi, 0, 0)),
            pl.BlockSpec((d, LANE), lambda bi, t: (0, 0)),
            pl.BlockSpec((1, LANE), lambda bi, t: (0, 0)),
        ],
        out_specs=pl.BlockSpec((1, tm, LANE), lambda bi, t: (bi, t, 0)),
        out_shape=jax.ShapeDtypeStruct((batch, seq, LANE), F32),
        scratch_shapes=[pltpu.VMEM((1, LANE), F32)],
        compiler_params=_params("parallel", "arbitrary"),
        name="decay_cumsum",
    )(x2, g.reshape(1, d), shift, scale, w_f, b_f)


def _out_proj_kernel(o_ref, w_ref, x_ref, gt_ref, y_ref):
    for c in range(0, y_ref.shape[1], MXU_COLS):
        cols = slice(c, c + MXU_COLS)
        y = jnp.dot(o_ref[...], w_ref[:, cols], preferred_element_type=F32)
        y_ref[:, cols] = x_ref[:, cols] + gt_ref[0][:, cols] * y


def _out_proj(o2, w, x2, gate, seq):
    n, wdim = o2.shape
    d = w.shape[1]
    tm = min(512, seq)
    tn = min(1024, d)
    tiles_per_batch = seq // tm
    return pl.pallas_call(
        _out_proj_kernel,
        grid=(n // tm, d // tn),
        in_specs=[
            pl.BlockSpec((tm, wdim), lambda i, j: (i, 0)),
            pl.BlockSpec((wdim, tn), lambda i, j: (0, j)),
            pl.BlockSpec((tm, tn), lambda i, j: (i, j)),
            pl.BlockSpec((1, 1, tn), lambda i, j: (i // tiles_per_batch, 0, j)),
        ],
        out_specs=pl.BlockSpec((tm, tn), lambda i, j: (i, j)),
        out_shape=jax.ShapeDtypeStruct((n, d), F32),
        compiler_params=_params("parallel", "parallel"),
        name="out_proj",
    )(o2, w, x2, gate)


def _top2_sum(a, b, c, d):
    hi1, lo1 = jnp.maximum(a, b), jnp.minimum(a, b)
    hi2, lo2 = jnp.maximum(c, d), jnp.minimum(c, d)
    return jnp.maximum(hi1, hi2) + jnp.maximum(jnp.minimum(hi1, hi2), jnp.maximum(lo1, lo2))


def _pick(idx, rows):
    out = rows[-1]
    for k in range(len(rows) - 2, -1, -1):
        out = jnp.where(idx == k, rows[k], out)
    return out


def _store_token_major(ref, mat):
    rows, d = mat.shape
    ds = d // LANE
    for s in range(ds):
        ref[pl.ds(s, rows, stride=ds), :] = mat[:, s * LANE:(s + 1) * LANE]


def _load_token_major(ref, rows, pitch, s):
    return ref[pl.ds(s, rows, stride=pitch), :]


def _gather_pitch(ds):
    pitch = -(-ds // 8) * 8
    return pitch if (pitch // 8) % 2 else pitch + 8


def _router_kernel(x_ref, g_ref, sh_ref, sc_ref, wrt_ref, rb_ref, h_ref, eid_ref, wt_ref, *, n_experts):
    h = _rms_mod(x_ref[...], g_ref[...], sh_ref[0], sc_ref[0])
    _store_token_major(h_ref, h)

    logits = lax.dot_general(wrt_ref[...], h, (((1,), (1,)), ((), ())), precision=HIGHEST, preferred_element_type=F32)
    prob = jax.nn.sigmoid(logits)
    biased = prob + rb_ref[...]
    epg = n_experts // N_GROUPS
    p_rows = [prob[e:e + 1, :] for e in range(n_experts)]
    b_rows = [biased[e:e + 1, :] for e in range(n_experts)]

    scores = [_top2_sum(*b_rows[gi * epg:(gi + 1) * epg]) for gi in range(N_GROUPS)]
    grp = jnp.zeros_like(scores[0], dtype=jnp.int32)
    best = scores[0]
    for gi in range(1, N_GROUPS):
        better = scores[gi] > best
        grp = jnp.where(better, gi, grp)
        best = jnp.where(better, scores[gi], best)

    in_b = [_pick(grp, [b_rows[gi * epg + j] for gi in range(N_GROUPS)]) for j in range(epg)]
    in_p = [_pick(grp, [p_rows[gi * epg + j] for gi in range(N_GROUPS)]) for j in range(epg)]

    loc0 = jnp.zeros_like(grp)
    top = in_b[0]
    for j in range(1, epg):
        better = in_b[j] > top
        loc0 = jnp.where(better, j, loc0)
        top = jnp.where(better, in_b[j], top)
    loc1 = jnp.full_like(grp, -1)
    second = jnp.full_like(top, NEG_INF)
    for j in range(epg):
        better = (loc0 != j) & ((loc1 < 0) | (in_b[j] > second))
        loc1 = jnp.where(better, j, loc1)
        second = jnp.where(better, in_b[j], second)

    w0 = _pick(loc0, in_p)
    w1 = _pick(loc1, in_p)
    denom = w0 + w1
    pad_i = jnp.zeros((eid_ref.shape[0] - TOP_K, grp.shape[1]), jnp.int32)
    eid_ref[...] = jnp.concatenate([grp * epg + loc0, grp * epg + loc1, pad_i], axis=0)
    wt_ref[...] = jnp.concatenate([w0 / denom, w1 / denom, pad_i.astype(F32)], axis=0)


def _router(x2, g, shift, scale, w_router_t, router_bias, seq):
    n, d = x2.shape
    e = w_router_t.shape[0]
    ds = d // LANE
    tm = min(256, seq)
    tiles_per_batch = seq // tm
    kern = functools.partial(_router_kernel, n_experts=e)
    return pl.pallas_call(
        kern,
        grid=(n // tm,),
        in_specs=[
            pl.BlockSpec((tm, d), lambda i: (i, 0)),
            pl.BlockSpec((1, d), lambda i: (0, 0)),
            pl.BlockSpec((1, 1, d), lambda i: (i // tiles_per_batch, 0, 0)),
            pl.BlockSpec((1, 1, d), lambda i: (i // tiles_per_batch, 0, 0)),
            pl.BlockSpec((e, d), lambda i: (0, 0)),
            pl.BlockSpec((e, 1), lambda i: (0, 0)),
        ],
        out_specs=[
            pl.BlockSpec((tm * ds, LANE), lambda i: (i, 0)),
            pl.BlockSpec((8, tm), lambda i: (0, i)),
            pl.BlockSpec((8, tm), lambda i: (0, i)),
        ],
        out_shape=[
            jax.ShapeDtypeStruct((n * ds, LANE), F32),
            jax.ShapeDtypeStruct((8, n), jnp.int32),
            jax.ShapeDtypeStruct((8, n), F32),
        ],
        compiler_params=_params("parallel"),
        name="moe_router",
    )(x2, g.reshape(1, d), shift, scale, w_router_t, router_bias.reshape(e, 1))


def _gather_rows(idx_of, src_hbm, dst, sem, count, ds):
    pitch = _gather_pitch(ds)

    def issue(r, carry):
        src = pl.ds(pl.multiple_of(idx_of(r) * ds, ds), ds)
        pltpu.make_async_copy(src_hbm.at[src], dst.at[pl.ds(pl.multiple_of(r * pitch, 8), ds)], sem).start()
        return carry

    lax.fori_loop(0, count, issue, 0, unroll=8)


def _wait_rows(src_hbm, dst, sem, count, ds):
    moved = pl.ds(0, count * ds)
    pltpu.make_async_copy(src_hbm.at[moved], dst.at[moved], sem).wait()


def _expert_kernel(row_tok_ref, blk_exp_ref, n_used_ref, h_hbm, w_in_ref, w_out_ref, y_ref, buf, xs_scr, sem, *, d_ff):
    i = pl.program_id(0)
    tm, d = xs_scr.shape
    ds = d // LANE
    pitch = _gather_pitch(ds)
    n_used = n_used_ref[0]
    slot = lax.bitwise_and(i, 1)

    def gather(blk, into):
        _gather_rows(lambda r: row_tok_ref[blk * tm + r], h_hbm, buf.at[into], sem.at[into], tm, ds)

    @pl.when(i == 0)
    def _():
        gather(0, 0)

    @pl.when(i < n_used)
    def _():
        _wait_rows(h_hbm, buf.at[slot], sem.at[slot], tm, ds)
        nxt = jnp.minimum(i + 1, n_used - 1)
        for r in range(tm):
            src = pl.ds(pl.multiple_of(row_tok_ref[nxt * tm + r] * ds, ds), ds)
            pltpu.make_async_copy(h_hbm.at[src], buf.at[1 - slot, pl.ds(r * pitch, ds)], sem.at[1 - slot]).start()
        for s in range(ds):
            xs_scr[:, s * LANE:(s + 1) * LANE] = _load_token_major(buf.at[slot], tm, pitch, s).astype(BF16)
        a = jnp.dot(xs_scr[...], w_in_ref[0, 0], preferred_element_type=F32)
        gate, up = a[:, :d_ff], a[:, d_ff:]
        act = (gate * jax.nn.sigmoid(gate) * up).astype(BF16)
        _store_token_major(y_ref, jnp.dot(act, w_out_ref[0, 0], preferred_element_type=F32))

        @pl.when(i + 1 >= n_used)
        def _():
            _wait_rows(h_hbm, buf.at[1 - slot], sem.at[1 - slot], tm, ds)

    @pl.when(i >= n_used)
    def _():
        y_ref[...] = jnp.zeros_like(y_ref)


def _experts(h_tm, row_tok, blk_exp, n_used, w_in, w_out, layer):
    _, e, d, f2 = w_in.shape
    ds = d // LANE
    d_ff = f2 // 2
    n_rows = row_tok.shape[0]
    tm = MOE_ROW_BLOCK
    n_blk = n_rows // tm
    kern = functools.partial(_expert_kernel, d_ff=d_ff)
    grid_spec = pltpu.PrefetchScalarGridSpec(
        num_scalar_prefetch=3,
        grid=(n_blk,),
        in_specs=[
            pl.BlockSpec(memory_space=pl.ANY),
            pl.BlockSpec((1, 1, d, f2), lambda i, rt, be, nu: (layer, be[i], 0, 0)),
            pl.BlockSpec((1, 1, d_ff, d), lambda i, rt, be, nu: (layer, be[i], 0, 0)),
        ],
        out_specs=pl.BlockSpec((tm * ds, LANE), lambda i, rt, be, nu: (i, 0)),
        scratch_shapes=[
            pltpu.VMEM((2, tm * _gather_pitch(ds), LANE), F32),
            pltpu.VMEM((tm, d), BF16),
            pltpu.SemaphoreType.DMA((2,)),
        ],
    )
    return pl.pallas_call(
        kern,
        grid_spec=grid_spec,
        out_shape=jax.ShapeDtypeStruct((n_rows * ds, LANE), F32),
        compiler_params=_params("arbitrary"),
        name="moe_experts",
    )(row_tok, blk_exp, n_used, h_tm, w_in, w_out)


def _combine_kernel(pos_ref, y_hbm, x_ref, wt_ref, gt_ref, o_ref, buf, sem):
    i = pl.program_id(0)
    tm, d = x_ref.shape
    ds = d // LANE
    slot = lax.bitwise_and(i, 1)

    def gather(tile, into):
        for k in range(TOP_K):
            _gather_rows(lambda r, k=k: pos_ref[TOP_K * (tile * tm + r) + k], y_hbm, buf.at[into, k],
                         sem.at[into, k], tm, ds)

    @pl.when(i == 0)
    def _():
        gather(0, 0)

    pitch = _gather_pitch(ds)
    for k in range(TOP_K):
        _wait_rows(y_hbm, buf.at[slot, k], sem.at[slot, k], tm, ds)
    last = pl.num_programs(0) - 1
    nxt = jnp.minimum(i + 1, last)
    for r in range(tm):
        for k in range(TOP_K):
            src = pl.ds(pl.multiple_of(pos_ref[TOP_K * (nxt * tm + r) + k] * ds, ds), ds)
            pltpu.make_async_copy(y_hbm.at[src], buf.at[1 - slot, k, pl.ds(r * pitch, ds)],
                                  sem.at[1 - slot, k]).start()
    w0 = wt_ref[:, 0:1]
    w1 = wt_ref[:, 1:2]
    for s in range(ds):
        cols = slice(s * LANE, (s + 1) * LANE)
        moe = (_load_token_major(buf.at[slot, 0], tm, pitch, s) * w0
               + _load_token_major(buf.at[slot, 1], tm, pitch, s) * w1)
        o_ref[:, cols] = x_ref[:, cols] + gt_ref[0][:, cols] * moe

    @pl.when(i == last)
    def _():
        for k in range(TOP_K):
            _wait_rows(y_hbm, buf.at[1 - slot, k], sem.at[1 - slot, k], tm, ds)


def _combine(y_tm, pos, x2, wts, gate, seq):
    n, d = x2.shape
    ds = d // LANE
    tm = min(256, seq)
    tiles_per_batch = seq // tm
    grid_spec = pltpu.PrefetchScalarGridSpec(
        num_scalar_prefetch=1,
        grid=(n // tm,),
        in_specs=[
            pl.BlockSpec(memory_space=pl.ANY),
            pl.BlockSpec((tm, d), lambda i, p: (i, 0)),
            pl.BlockSpec((tm, TOP_K), lambda i, p: (i, 0)),
            pl.BlockSpec((1, 1, d), lambda i, p: (i // tiles_per_batch, 0, 0)),
        ],
        out_specs=pl.BlockSpec((tm, d), lambda i, p: (i, 0)),
        scratch_shapes=[
            pltpu.VMEM((2, TOP_K, tm * _gather_pitch(ds), LANE), F32),
            pltpu.SemaphoreType.DMA((2, TOP_K)),
        ],
    )
    return pl.pallas_call(
        _combine_kernel,
        grid_spec=grid_spec,
        out_shape=jax.ShapeDtypeStruct((n, d), F32),
        compiler_params=_params("arbitrary"),
        name="moe_combine",
    )(pos, y_tm, x2, wts, gate)


def _moe_layer(x2, g, shift, scale, gate, w_router_t, router_bias, w_in, w_out, layer, seq):
    n, d = x2.shape
    e = w_router_t.shape[0]
    h_tm, eid8, wt8 = _router(x2, g, shift, scale, w_router_t, router_bias, seq)
    eid = eid8[:TOP_K].T
    wts = wt8[:TOP_K].T

    rb = MOE_ROW_BLOCK
    nk = n * TOP_K
    e_flat = eid.reshape(nk)
    one_hot = (e_flat[:, None] == jnp.arange(e, dtype=jnp.int32)[None, :]).astype(jnp.int32)
    running = jnp.cumsum(one_hot, axis=0)
    counts = running[-1]
    rank = jnp.sum(running * one_hot, axis=1) - 1
    padded = ((counts + rb - 1) // rb) * rb
    ends = jnp.cumsum(padded)
    pos = ((ends - padded)[e_flat] + rank).astype(jnp.int32)
    n_rows = ((nk + e * (rb - 1) + rb - 1) // rb) * rb
    n_blk = n_rows // rb
    tok = jnp.arange(nk, dtype=jnp.int32) // TOP_K
    row_tok = jnp.zeros((n_rows,), jnp.int32).at[pos].set(tok)
    blk_start = jnp.arange(n_blk, dtype=jnp.int32) * rb
    blk_exp = jnp.sum((ends[None, :] <= blk_start[:, None]).astype(jnp.int32), axis=1)
    blk_exp = jnp.minimum(blk_exp, e - 1).astype(jnp.int32)
    n_used = (ends[-1] // rb).astype(jnp.int32).reshape(1)

    y_tm = _experts(h_tm, row_tok, blk_exp, n_used, w_in, w_out, layer)
    return _combine(y_tm, pos, x2, wts, gate, seq)


def kernel(x, c, g_attn, g_ffn, w_mod, b_mod, w_qkv_a, g_q_a, g_k_a, w_o_a, g_kv, w_mod_kv, b_mod_kv, w_kvf, b_f,
           g_k_b, w_q_b, g_q_b, w_o_b, w_router, router_bias, w_in, w_out):
    b, s, d = x.shape
    n = b * s
    depth = g_attn.shape[0]
    n_a = w_qkv_a.shape[0]
    n_heads = b_f.shape[0]
    hd = g_q_a.shape[-1]
    w = n_heads * hd
    assert hd == LANE and s % MOBA_BLOCK == 0 and d % LANE == 0

    c_pad = jnp.pad(c, ((0, 8 - b), (0, 0)))
    mod = _mod_call(c_pad, w_mod, b_mod)[:, :b]
    mod_kv = _mod_call(c_pad, w_mod_kv[None], b_mod_kv[None])[0, :b]
    slopes = jnp.exp2(-8.0 * jnp.arange(1, n_heads + 1, dtype=F32) / n_heads)
    slope_pieces = jnp.stack(_bf16_pieces(slopes * LOG2E), axis=1).reshape(3 * n_heads)
    q_scale = LOG2E * hd ** -0.5

    def logit_bound(g_q, g_k):
        bound = 1.02 * LOG2E * hd ** 0.5 * jnp.max(jnp.abs(g_q)) * jnp.max(jnp.abs(g_k))
        return bound.astype(F32).reshape(1)
    w_router_t = w_router.T
    ones_hd = jnp.ones((hd,), F32)
    w_in_bf = w_in.astype(BF16)
    w_out_bf = w_out.astype(BF16)

    x2 = x.reshape(n, d)
    kv = cum = None
    for l in range(depth):
        sh_a, sc_a, gt_a, sh_m, sc_m, gt_m = (mod[l][:, k * d:(k + 1) * d].reshape(b, 1, d) for k in range(6))
        if l < n_a:
            gains = jnp.stack([g_q_a[l] * q_scale, g_k_a[l], ones_hd]).reshape(3, 1, hd)
            qkv = _norm_proj(x2, g_attn[l], sh_a, sc_a, w_qkv_a[l].astype(BF16), gains, 2, s)
            o = _moba(qkv.reshape(b, s, 3 * w), logit_bound(g_q_a[l], g_k_a[l]), slope_pieces, n_heads)
            w_o = w_o_a[l]
        else:
            j = l - n_a
            gains = (g_q_b[j] * q_scale).reshape(1, 1, hd)
            q = _norm_proj(x2, g_attn[l], sh_a, sc_a, w_q_b[j].astype(BF16), gains, 1, s)
            o = _fox(q.reshape(b, s, w), kv, cum, logit_bound(g_q_b[j], g_k_b), n_heads)
            w_o = w_o_b[j]
        x2 = _out_proj(o.reshape(n, w), w_o.astype(BF16), x2, gt_a, s)
        x2 = _moe_layer(x2, g_ffn[l], sh_m, sc_m, gt_m, w_router_t, router_bias, w_in_bf, w_out_bf, l, s)
        if l == n_a - 1:
            sh_kv = mod_kv[:, :d].reshape(b, 1, d)
            sc_kv = mod_kv[:, d:].reshape(b, 1, d)
            gains = jnp.stack([g_k_b, ones_hd]).reshape(2, 1, hd)
            kv = _norm_proj(x2, g_kv, sh_kv, sc_kv, w_kvf[:, :2 * w].astype(BF16), gains, 1, s).reshape(b, s, 2 * w)
            w_f = jnp.pad(w_kvf[:, 2 * w:], ((0, 0), (0, LANE - n_heads)))
            b_f_pad = jnp.pad(b_f, (0, LANE - n_heads)).reshape(1, LANE)
            cum3 = _decay_cumsum(x2, g_kv, sh_kv, sc_kv, w_f, b_f_pad, b, s)
            cum = cum3[:, :, :n_heads].transpose(0, 2, 1)
    return x2.reshape(b, s, d)
```

```python
import functools
import math

import jax
import jax.numpy as jnp
from jax import lax
from jax.experimental import pallas as pl
from jax.experimental.pallas import tpu as pltpu

MOBA_BLOCK = 256
MOBA_TOPK = 3
N_GROUPS = 4
TOP_K = 2
EPS = 1e-6

LANE = 128
MXU_COLS = 256
V7X_VMEM_BYTES = 64 * 1024 * 1024
VMEM_LIMIT = V7X_VMEM_BYTES - 8 * 1024 * 1024

MOE_ROW_BLOCK = 256
ATTN_BLOCK = 256
ATTN_HEADS = 2
HIGHEST = lax.Precision.HIGHEST
BF16 = jnp.bfloat16
F32 = jnp.float32
NEG_INF = float("-inf")
MASKED = -1e30
LOG2E = math.log2(math.e)


def _params(*semantics):
    return pltpu.CompilerParams(dimension_semantics=semantics, vmem_limit_bytes=VMEM_LIMIT)


def _rms_mod(x, g, shift, scale):
    ms = jnp.mean(x * x, axis=-1, keepdims=True)
    y = x * lax.rsqrt(ms + EPS) * g
    return y * (1.0 + scale) + shift


def _bf16_pieces(x):
    hi = x.astype(BF16).astype(F32)
    mid = (x - hi).astype(BF16).astype(F32)
    lo = (x - hi - mid).astype(BF16).astype(F32)
    return hi, mid, lo


def _mod_kernel(c_ref, w_ref, b_ref, o_ref):
    c = c_ref[...]
    cond = c * jax.nn.sigmoid(c)
    o_ref[0] = jnp.dot(cond, w_ref[0], precision=HIGHEST, preferred_element_type=F32) + b_ref[0]


def _mod_call(c_pad, w, b):
    n_l, d, m = w.shape
    rows = c_pad.shape[0]
    tn = min(1024, m)
    return pl.pallas_call(
        _mod_kernel,
        grid=(n_l, m // tn),
        in_specs=[
            pl.BlockSpec((rows, d), lambda l, j: (0, 0)),
            pl.BlockSpec((1, d, tn), lambda l, j: (l, 0, j)),
            pl.BlockSpec((1, 1, tn), lambda l, j: (l, 0, j)),
        ],
        out_specs=pl.BlockSpec((1, rows, tn), lambda l, j: (l, 0, j)),
        out_shape=jax.ShapeDtypeStruct((n_l, rows, m), F32),
        compiler_params=_params("parallel", "parallel"),
        name="adaln_mod",
    )(c_pad, w, b.reshape(n_l, 1, m))


def _norm_proj_kernel(x_ref, g_ref, sh_ref, sc_ref, w_ref, hg_ref, o_ref, h_scr, *, n_norm_tiles, n_col_tiles):
    j = pl.program_id(1)

    @pl.when(j == 0)
    def _():
        h_scr[...] = _rms_mod(x_ref[...], g_ref[...], sh_ref[0], sc_ref[0]).astype(BF16)

    def project(head_normed):
        hg = hg_ref[0]
        for c in range(0, o_ref.shape[1], MXU_COLS):
            y = jnp.dot(h_scr[...], w_ref[:, c:c + MXU_COLS], preferred_element_type=F32)
            if not head_normed:
                o_ref[:, c:c + MXU_COLS] = y.astype(o_ref.dtype)
                continue
            for hh in range(0, MXU_COLS, LANE):
                yh = y[:, hh:hh + LANE]
                ms = jnp.mean(yh * yh, axis=-1, keepdims=True)
                o_ref[:, c + hh:c + hh + LANE] = (yh * lax.rsqrt(ms + EPS) * hg).astype(o_ref.dtype)

    if n_norm_tiles == n_col_tiles:
        project(True)
    else:
        pl.when(j < n_norm_tiles)(functools.partial(project, True))
        pl.when(j >= n_norm_tiles)(functools.partial(project, False))


def _norm_proj(x2, g, shift, scale, w, head_gain, n_norm_sections, seq):
    n, d = x2.shape
    m = w.shape[1]
    n_sections = head_gain.shape[0]
    sec_w = m // n_sections
    tm = min(512, seq)
    tn = min(1024, sec_w)
    assert tn % MXU_COLS == 0
    tiles_per_batch = seq // tm
    tiles_per_section = sec_w // tn
    n_col_tiles = m // tn
    kern = functools.partial(_norm_proj_kernel, n_norm_tiles=n_norm_sections * tiles_per_section,
                             n_col_tiles=n_col_tiles)
    return pl.pallas_call(
        kern,
        grid=(n // tm, n_col_tiles),
        in_specs=[
            pl.BlockSpec((tm, d), lambda i, j: (i, 0)),
            pl.BlockSpec((1, d), lambda i, j: (0, 0)),
            pl.BlockSpec((1, 1, d), lambda i, j: (i // tiles_per_batch, 0, 0)),
            pl.BlockSpec((1, 1, d), lambda i, j: (i // tiles_per_batch, 0, 0)),
            pl.BlockSpec((d, tn), lambda i, j: (0, j)),
            pl.BlockSpec((1, 1, LANE), lambda i, j: (j // tiles_per_section, 0, 0)),
        ],
        out_specs=pl.BlockSpec((tm, tn), lambda i, j: (i, j)),
        out_shape=jax.ShapeDtypeStruct((n, m), BF16),
        scratch_shapes=[pltpu.VMEM((tm, d), BF16)],
        compiler_params=_params("parallel", "arbitrary"),
        name="norm_proj",
    )(x2, g.reshape(1, d), shift, scale, w, head_gain)


_MAX_FIXED_SHIFT = 40.0


def _causal(rows, width):
    row = lax.broadcasted_iota(jnp.int32, (rows, width), 0)
    col = lax.broadcasted_iota(jnp.int32, (rows, width), 1)
    return row >= col - (width - rows)


def _flash_fixed(i, q_wides, kp_scr, vp_scr, s_scr, acc_scr):
    tq = ATTN_BLOCK
    half = 2 * tq
    trip_keys = 2 * half
    heads = range(len(q_wides))
    trips = lax.shift_right_logical(i, 2)

    def scores(hh, start, size):
        return lax.dot_general(q_wides[hh], kp_scr[hh, pl.ds(start, size), :], (((1,), (1,)), ((), ())),
                               preferred_element_type=F32)

    def weighted_values(hh, p, start, size):
        return jnp.dot(p, vp_scr[hh, pl.ds(start, size), :], preferred_element_type=F32)

    def last_blocks(n_blk):
        size = n_blk * tq
        start = pl.multiple_of((i - (n_blk - 1)) * tq, tq)
        for hh in heads:
            s_scr[hh] = scores(hh, 0, half)
        for hh in heads:
            s = jnp.where(_causal(tq, size), scores(hh, start, size), MASKED)
            acc_scr[hh] = weighted_values(hh, jnp.exp2(s).astype(BF16), start, size)

    lax.switch(lax.bitwise_and(i, 3), [functools.partial(last_blocks, n) for n in (1, 2, 3, 4)])

    def body(t, carry):
        base = pl.multiple_of(t * trip_keys, trip_keys)
        ahead = pl.multiple_of(jnp.minimum(t + 1, trips - 1) * trip_keys, trip_keys)
        second_s = [scores(hh, base + half, half) for hh in heads]
        first_p = [jnp.exp2(s_scr[hh]).astype(BF16) for hh in heads]
        for hh in heads:
            s_scr[hh] = scores(hh, ahead, half)
        for hh in heads:
            p = jnp.concatenate([first_p[hh], jnp.exp2(second_s[hh]).astype(BF16)], axis=1)
            acc_scr[hh] += weighted_values(hh, p, base, trip_keys)
        return carry

    lax.fori_loop(0, trips, body, 0)
    return [acc_scr[hh] for hh in heads]


def _flash_online(i, q_wides, kp_scr, vp_scr):
    tq = ATTN_BLOCK
    chunk = 2 * tq
    heads = range(len(q_wides))
    n_chunks = lax.shift_right_logical(i, 1)

    def scores(hh, start, size):
        return lax.dot_general(q_wides[hh], kp_scr[hh, pl.ds(start, size), :], (((1,), (1,)), ((), ())),
                               preferred_element_type=F32)

    def weighted_values(hh, p, start, size):
        return jnp.dot(p.astype(BF16), vp_scr[hh, pl.ds(start, size), :], preferred_element_type=F32)

    def chunk_logits(start):
        out = []
        for hh in heads:
            s = scores(hh, start, chunk)
            out.append((s, jnp.max(s, axis=-1, keepdims=True)))
        return tuple(out)

    def diagonal(start, size):
        ahead = chunk_logits(0)
        ms, accs = [], []
        for hh in heads:
            s = jnp.where(_causal(tq, size), scores(hh, start, size), MASKED)
            m = jnp.max(s, axis=-1, keepdims=True)
            ms.append(m)
            accs.append(weighted_values(hh, jnp.exp2(s - m), start, size))
        return tuple(ms), tuple(accs), ahead

    carry = lax.cond(lax.bitwise_and(i, 1) == 1,
                     lambda: diagonal(pl.multiple_of((i - 1) * tq, tq), chunk),
                     lambda: diagonal(pl.multiple_of(i * tq, tq), tq))

    def body(jj, carry):
        ms, accs, now = carry
        start = pl.multiple_of(jj * chunk, chunk)
        ahead = chunk_logits(pl.multiple_of(jnp.minimum(jj + 1, n_chunks - 1) * chunk, chunk))
        new_ms, new_accs = [], []
        for hh in heads:
            s, s_max = now[hh]
            m_new = jnp.maximum(ms[hh], s_max)
            p = jnp.exp2(s - m_new)
            new_ms.append(m_new)
            new_accs.append(jnp.exp2(ms[hh] - m_new) * accs[hh] + weighted_values(hh, p, start, chunk))
        return tuple(new_ms), tuple(new_accs), ahead

    _, accs, _ = lax.fori_loop(0, n_chunks, body, carry)
    return accs


def _head_cols(hh):
    return slice(hh * LANE, (hh + 1) * LANE)


def _widen_values(v_ref, vp_scr, n_blocks):
    tq = ATTN_BLOCK
    ones_col = jnp.where(lax.broadcasted_iota(jnp.int32, (tq, LANE), 1) == 0, 1.0, 0.0).astype(BF16)

    def fill(jb, carry):
        rows = pl.ds(pl.multiple_of(jb * tq, tq), tq)
        for hh in range(ATTN_HEADS):
            vp_scr[hh, rows, 0:LANE] = v_ref[0, rows, _head_cols(hh)]
            vp_scr[hh, rows, LANE:2 * LANE] = ones_col
        return carry

    lax.fori_loop(0, n_blocks, fill, 0)


def _attend(i, q_wides, kp_scr, vp_scr, extra_scr, o_ref):
    if extra_scr:
        accs = _flash_fixed(i, q_wides, kp_scr, vp_scr, *extra_scr)
    else:
        accs = _flash_online(i, q_wides, kp_scr, vp_scr)
    for hh, acc in enumerate(accs):
        o_ref[0, :, _head_cols(hh)] = (acc[:, :LANE] / acc[:, LANE:LANE + 1]).astype(o_ref.dtype)


def _attn_scratch(seq, fixed_shift):
    scratch = [
        pltpu.VMEM((ATTN_HEADS, seq, 2 * LANE), BF16),
        pltpu.VMEM((ATTN_HEADS, seq, 2 * LANE), BF16),
    ]
    if fixed_shift:
        scratch.append(pltpu.VMEM((ATTN_HEADS, ATTN_BLOCK, 2 * ATTN_BLOCK), F32))
        scratch.append(pltpu.VMEM((ATTN_HEADS, ATTN_BLOCK, 2 * LANE), F32))
    return scratch


def _by_shift(shift, call):
    return lax.cond(shift[0] <= _MAX_FIXED_SHIFT, functools.partial(call, True), functools.partial(call, False))


_MOBA_MAX_BLOCKS = 16


def _moba_kernel(shift_ref, slopes_ref, q_ref, k_ref, v_ref, o_ref, kmean_scr, kp_scr, vp_scr, *extra_scr, n_blocks):
    hp = pl.program_id(1)
    i = pl.program_id(2)
    mb = MOBA_BLOCK
    lane = lax.broadcasted_iota(jnp.int32, (mb, LANE), 1)
    pos = lax.broadcasted_iota(jnp.int32, (mb, LANE), 0).astype(F32)

    def slope_lanes(hh, base, sign, init):
        out = init
        for p in range(3):
            piece = slopes_ref[3 * (ATTN_HEADS * hp + hh) + p]
            out = jnp.where((lane == base + p) | (lane == base + 3 + p), sign * piece, out)
        return out

    @pl.when(i == 0)
    def _():
        kmean_scr[...] = jnp.zeros_like(kmean_scr)
        _widen_values(v_ref, vp_scr, n_blocks)

        def fill(jb, carry):
            rows = pl.ds(pl.multiple_of(jb * mb, mb), mb)
            shared = jnp.where(lane == jb, 1.0, 0.0)
            shared = jnp.where((lane >= 22) & (lane < 25), pos, shared)
            shared = jnp.where((lane >= 25) & (lane < 28), jnp.asarray(jb * mb, F32), shared)
            shared = jnp.where(lane == 28, 1.0, shared)
            for hh in range(ATTN_HEADS):
                kb = k_ref[0, rows, _head_cols(hh)]
                kmean_scr[hh, pl.ds(jb, 1), :] = jnp.mean(kb.astype(F32), axis=0, keepdims=True)
                kp_scr[hh, rows, 0:LANE] = kb
                kp_scr[hh, rows, LANE:2 * LANE] = slope_lanes(hh, 16, -1.0, shared).astype(BF16)
            return carry

        lax.fori_loop(0, n_blocks, fill, 0)

    q_wides = []
    blk = lax.broadcasted_iota(jnp.int32, (_MOBA_MAX_BLOCKS, mb), 0)
    for hh in range(ATTN_HEADS):
        q = q_ref[0, :, _head_cols(hh)]
        gate = lax.dot_general(kmean_scr[hh], q.astype(F32), (((1,), (1,)), ((), ())),
                               precision=HIGHEST, preferred_element_type=F32)
        avail = blk < i
        open_blk = blk == i
        for _ in range(MOBA_TOPK):
            g = jnp.where(avail, gate, NEG_INF)
            gmax = jnp.max(g, axis=0, keepdims=True)
            first = jnp.min(jnp.where(avail & (g == gmax), blk, _MOBA_MAX_BLOCKS), axis=0, keepdims=True)
            pick = blk == first
            open_blk = open_blk | pick
            avail = avail & jnp.logical_not(pick)
        mask_t = jnp.concatenate([jnp.where(open_blk, 0.0, MASKED),
                                  jnp.zeros((LANE - _MOBA_MAX_BLOCKS, mb), F32)], axis=0)

        aug = mask_t.T
        aug = jnp.where((lane >= 16) & (lane < 19), pos, aug)
        aug = jnp.where((lane >= 19) & (lane < 22), jnp.asarray(i * mb, F32), aug)
        aug = slope_lanes(hh, 22, 1.0, aug)
        aug = jnp.where(lane == 28, -shift_ref[0], aug)
        q_wides.append(jnp.concatenate([q, aug.astype(BF16)], axis=1))
    _attend(i, q_wides, kp_scr, vp_scr, extra_scr, o_ref)


def _moba(qkv, shift, slope_pieces, n_heads):
    b, s, w3 = qkv.shape
    w = w3 // 3
    hd = w // n_heads
    nb = s // MOBA_BLOCK
    assert nb <= _MOBA_MAX_BLOCKS and hd == LANE and MOBA_BLOCK == ATTN_BLOCK
    assert n_heads % ATTN_HEADS == 0 and s >= 2 * ATTN_BLOCK
    kern = functools.partial(_moba_kernel, n_blocks=nb)
    groups = n_heads // ATTN_HEADS
    gw = ATTN_HEADS * hd

    def call(fixed_shift):
        return pl.pallas_call(
            kern,
            grid=(b, groups, nb),
            in_specs=[
                pl.BlockSpec(memory_space=pltpu.SMEM),
                pl.BlockSpec(memory_space=pltpu.SMEM),
                pl.BlockSpec((1, MOBA_BLOCK, gw), lambda bi, hp, i: (bi, i, hp)),
                pl.BlockSpec((1, s, gw), lambda bi, hp, i: (bi, 0, groups + hp)),
                pl.BlockSpec((1, s, gw), lambda bi, hp, i: (bi, 0, 2 * groups + hp)),
            ],
            out_specs=pl.BlockSpec((1, MOBA_BLOCK, gw), lambda bi, hp, i: (bi, i, hp)),
            out_shape=jax.ShapeDtypeStruct((b, s, w), BF16),
            scratch_shapes=[pltpu.VMEM((ATTN_HEADS, _MOBA_MAX_BLOCKS, hd), F32)] + _attn_scratch(s, fixed_shift),
            compiler_params=_params("parallel", "parallel", "arbitrary"),
            name="moba_attention" if fixed_shift else "moba_attention_online",
        )(shift, slope_pieces, qkv, qkv, qkv)

    return _by_shift(shift, call)


def _fox_kernel(shift_ref, q_ref, k_ref, v_ref, cum_ref, o_ref, qa_scr, kp_scr, vp_scr, *extra_scr, n_blocks):
    i = pl.program_id(2)
    tq = ATTN_BLOCK
    lane = lax.broadcasted_iota(jnp.int32, (tq, LANE), 1)

    @pl.when(i == 0)
    def _():
        _widen_values(v_ref, vp_scr, n_blocks)

        def fill(jb, carry):
            rows = pl.ds(pl.multiple_of(jb * tq, tq), tq)
            for hh in range(ATTN_HEADS):
                q_side = jnp.where((lane >= 3) & (lane < 6), 1.0, jnp.where(lane == 6, -shift_ref[0], 0.0))
                k_side = jnp.where((lane < 3) | (lane == 6), 1.0, 0.0)
                for p, piece in enumerate(_bf16_pieces(cum_ref[0, hh, rows, :] * LOG2E)):
                    q_side = jnp.where(lane == p, piece, q_side)
                    k_side = jnp.where(lane == 3 + p, -piece, k_side)
                qa_scr[hh, rows, :] = q_side.astype(BF16)
                kp_scr[hh, rows, 0:LANE] = k_ref[0, rows, _head_cols(hh)]
                kp_scr[hh, rows, LANE:2 * LANE] = k_side.astype(BF16)
            return carry

        lax.fori_loop(0, n_blocks, fill, 0)

    rows = pl.ds(pl.multiple_of(i * tq, tq), tq)
    q_wides = [jnp.concatenate([q_ref[0, :, _head_cols(hh)], qa_scr[hh, rows, :]], axis=1)
               for hh in range(ATTN_HEADS)]
    _attend(i, q_wides, kp_scr, vp_scr, extra_scr, o_ref)


def _fox(q, kv, cum, shift, n_heads):
    b, s, w = q.shape
    hd = w // n_heads
    tq = ATTN_BLOCK
    nt = s // tq
    assert hd == LANE and s % tq == 0 and n_heads % ATTN_HEADS == 0 and s >= 2 * tq
    kern = functools.partial(_fox_kernel, n_blocks=nt)
    groups = n_heads // ATTN_HEADS
    gw = ATTN_HEADS * hd

    def call(fixed_shift):
        return pl.pallas_call(
            kern,
            grid=(b, groups, nt),
            in_specs=[
                pl.BlockSpec(memory_space=pltpu.SMEM),
                pl.BlockSpec((1, tq, gw), lambda bi, hp, i: (bi, i, hp)),
                pl.BlockSpec((1, s, gw), lambda bi, hp, i: (bi, 0, hp)),
                pl.BlockSpec((1, s, gw), lambda bi, hp, i: (bi, 0, groups + hp)),
                pl.BlockSpec((1, ATTN_HEADS, s, 1), lambda bi, hp, i: (bi, hp, 0, 0)),
            ],
            out_specs=pl.BlockSpec((1, tq, gw), lambda bi, hp, i: (bi, i, hp)),
            out_shape=jax.ShapeDtypeStruct((b, s, w), BF16),
            scratch_shapes=[pltpu.VMEM((ATTN_HEADS, s, LANE), BF16)] + _attn_scratch(s, fixed_shift),
            compiler_params=_params("parallel", "parallel", "arbitrary"),
            name="fox_attention" if fixed_shift else "fox_attention_online",
        )(shift, q, kv, kv, cum.reshape(b, n_heads, s, 1))

    return _by_shift(shift, call)


def _decay_kernel(x_ref, g_ref, sh_ref, sc_ref, w_ref, b_ref, o_ref, carry_scr):
    t = pl.program_id(1)

    @pl.when(t == 0)
    def _():
        carry_scr[...] = jnp.zeros_like(carry_scr)

    z = _rms_mod(x_ref[...], g_ref[...], sh_ref[0], sc_ref[0])
    f = jnp.dot(z, w_ref[...], precision=HIGHEST, preferred_element_type=F32) + b_ref[...]
    log_f = jnp.minimum(f, 0.0) - jnp.log1p(jnp.exp(-jnp.abs(f)))
    tm = f.shape[0]
    lower = (lax.broadcasted_iota(jnp.int32, (tm, tm), 0) >= lax.broadcasted_iota(jnp.int32, (tm, tm), 1))
    cum = jnp.dot(jnp.where(lower, 1.0, 0.0), log_f, precision=HIGHEST, preferred_element_type=F32) + carry_scr[...]
    o_ref[0] = cum
    carry_scr[...] = cum[tm - 1:tm, :]


def _decay_cumsum(x2, g, shift, scale, w_f, b_f, batch, seq):
    n, d = x2.shape
    tm = min(512, seq)
    tiles = seq // tm
    return pl.pallas_call(
        _decay_kernel,
        grid=(batch, tiles),
        in_specs=[
            pl.BlockSpec((tm, d), lambda bi, t: (bi * tiles + t, 0)),
            pl.BlockSpec((1, d), lambda bi, t: (0, 0)),
            pl.BlockSpec((1, 1, d), lambda bi, t: (bi, 0, 0)),
            pl.BlockSpec((1, 1, d), lambda bi, t: (bi, 0, 0)),
            pl.BlockSpec((d, LANE), lambda bi, t: (0, 0)),
            pl.BlockSpec((1, LANE), lambda bi, t: (0, 0)),
        ],
        out_specs=pl.BlockSpec((1, tm, LANE), lambda bi, t: (bi, t, 0)),
        out_shape=jax.ShapeDtypeStruct((batch, seq, LANE), F32),
        scratch_shapes=[pltpu.VMEM((1, LANE), F32)],
        compiler_params=_params("parallel", "arbitrary"),
        name="decay_cumsum",
    )(x2, g.reshape(1, d), shift, scale, w_f, b_f)


def _out_proj_kernel(o_ref, w_ref, x_ref, gt_ref, y_ref):
    for c in range(0, y_ref.shape[1], MXU_COLS):
        cols = slice(c, c + MXU_COLS)
        y = jnp.dot(o_ref[...], w_ref[:, cols], preferred_element_type=F32)
        y_ref[:, cols] = x_ref[:, cols] + gt_ref[0][:, cols] * y


def _out_proj(o2, w, x2, gate, seq):
    n, wdim = o2.shape
    d = w.shape[1]
    tm = min(512, seq)
    tn = min(1024, d)
    tiles_per_batch = seq // tm
    return pl.pallas_call(
        _out_proj_kernel,
        grid=(n // tm, d // tn),
        in_specs=[
            pl.BlockSpec((tm, wdim), lambda i, j: (i, 0)),
            pl.BlockSpec((wdim, tn), lambda i, j: (0, j)),
            pl.BlockSpec((tm, tn), lambda i, j: (i, j)),
            pl.BlockSpec((1, 1, tn), lambda i, j: (i // tiles_per_batch, 0, j)),
        ],
        out_specs=pl.BlockSpec((tm, tn), lambda i, j: (i, j)),
        out_shape=jax.ShapeDtypeStruct((n, d), F32),
        compiler_params=_params("parallel", "parallel"),
        name="out_proj",
    )(o2, w, x2, gate)


def _top2_sum(a, b, c, d):
    hi1, lo1 = jnp.maximum(a, b), jnp.minimum(a, b)
    hi2, lo2 = jnp.maximum(c, d), jnp.minimum(c, d)
    return jnp.maximum(hi1, hi2) + jnp.maximum(jnp.minimum(hi1, hi2), jnp.maximum(lo1, lo2))


def _pick(idx, rows):
    out = rows[-1]
    for k in range(len(rows) - 2, -1, -1):
        out = jnp.where(idx == k, rows[k], out)
    return out


def _store_token_major(ref, mat):
    rows, d = mat.shape
    ds = d // LANE
    for s in range(ds):
        ref[pl.ds(s, rows, stride=ds), :] = mat[:, s * LANE:(s + 1) * LANE]


def _load_token_major(ref, rows, pitch, s):
    return ref[pl.ds(s, rows, stride=pitch), :]


def _gather_pitch(ds):
    pitch = -(-ds // 8) * 8
    return pitch if (pitch // 8) % 2 else pitch + 8


def _router_kernel(x_ref, g_ref, sh_ref, sc_ref, wrt_ref, rb_ref, h_ref, eid_ref, wt_ref, *, n_experts):
    h = _rms_mod(x_ref[...], g_ref[...], sh_ref[0], sc_ref[0])
    _store_token_major(h_ref, h)

    logits = lax.dot_general(wrt_ref[...], h, (((1,), (1,)), ((), ())), precision=HIGHEST, preferred_element_type=F32)
    prob = jax.nn.sigmoid(logits)
    biased = prob + rb_ref[...]
    epg = n_experts // N_GROUPS
    p_rows = [prob[e:e + 1, :] for e in range(n_experts)]
    b_rows = [biased[e:e + 1, :] for e in range(n_experts)]

    scores = [_top2_sum(*b_rows[gi * epg:(gi + 1) * epg]) for gi in range(N_GROUPS)]
    grp = jnp.zeros_like(scores[0], dtype=jnp.int32)
    best = scores[0]
    for gi in range(1, N_GROUPS):
        better = scores[gi] > best
        grp = jnp.where(better, gi, grp)
        best = jnp.where(better, scores[gi], best)

    in_b = [_pick(grp, [b_rows[gi * epg + j] for gi in range(N_GROUPS)]) for j in range(epg)]
    in_p = [_pick(grp, [p_rows[gi * epg + j] for gi in range(N_GROUPS)]) for j in range(epg)]

    loc0 = jnp.zeros_like(grp)
    top = in_b[0]
    for j in range(1, epg):
        better = in_b[j] > top
        loc0 = jnp.where(better, j, loc0)
        top = jnp.where(better, in_b[j], top)
    loc1 = jnp.full_like(grp, -1)
    second = jnp.full_like(top, NEG_INF)
    for j in range(epg):
        better = (loc0 != j) & ((loc1 < 0) | (in_b[j] > second))
        loc1 = jnp.where(better, j, loc1)
        second = jnp.where(better, in_b[j], second)

    w0 = _pick(loc0, in_p)
    w1 = _pick(loc1, in_p)
    denom = w0 + w1
    pad_i = jnp.zeros((eid_ref.shape[0] - TOP_K, grp.shape[1]), jnp.int32)
    eid_ref[...] = jnp.concatenate([grp * epg + loc0, grp * epg + loc1, pad_i], axis=0)
    wt_ref[...] = jnp.concatenate([w0 / denom, w1 / denom, pad_i.astype(F32)], axis=0)


def _router(x2, g, shift, scale, w_router_t, router_bias, seq):
    n, d = x2.shape
    e = w_router_t.shape[0]
    ds = d // LANE
    tm = min(256, seq)
    tiles_per_batch = seq // tm
    kern = functools.partial(_router_kernel, n_experts=e)
    return pl.pallas_call(
        kern,
        grid=(n // tm,),
        in_specs=[
            pl.BlockSpec((tm, d), lambda i: (i, 0)),
            pl.BlockSpec((1, d), lambda i: (0, 0)),
            pl.BlockSpec((1, 1, d), lambda i: (i // tiles_per_batch, 0, 0)),
            pl.BlockSpec((1, 1, d), lambda i: (i // tiles_per_batch, 0, 0)),
            pl.BlockSpec((e, d), lambda i: (0, 0)),
            pl.BlockSpec((e, 1), lambda i: (0, 0)),
        ],
        out_specs=[
            pl.BlockSpec((tm * ds, LANE), lambda i: (i, 0)),
            pl.BlockSpec((8, tm), lambda i: (0, i)),
            pl.BlockSpec((8, tm), lambda i: (0, i)),
        ],
        out_shape=[
            jax.ShapeDtypeStruct((n * ds, LANE), F32),
            jax.ShapeDtypeStruct((8, n), jnp.int32),
            jax.ShapeDtypeStruct((8, n), F32),
        ],
        compiler_params=_params("parallel"),
        name="moe_router",
    )(x2, g.reshape(1, d), shift, scale, w_router_t, router_bias.reshape(e, 1))


def _gather_rows(idx_of, src_hbm, dst, sem, count, ds):
    pitch = _gather_pitch(ds)

    def issue(r, carry):
        src = pl.ds(pl.multiple_of(idx_of(r) * ds, ds), ds)
        pltpu.make_async_copy(src_hbm.at[src], dst.at[pl.ds(pl.multiple_of(r * pitch, 8), ds)], sem).start()
        return carry

    lax.fori_loop(0, count, issue, 0, unroll=8)


def _wait_rows(src_hbm, dst, sem, count, ds):
    moved = pl.ds(0, count * ds)
    pltpu.make_async_copy(src_hbm.at[moved], dst.at[moved], sem).wait()


def _expert_kernel(row_tok_ref, blk_exp_ref, n_used_ref, h_hbm, w_in_ref, w_out_ref, y_ref, buf, xs_scr, sem, *, d_ff):
    i = pl.program_id(0)
    tm, d = xs_scr.shape
    ds = d // LANE
    pitch = _gather_pitch(ds)
    n_used = n_used_ref[0]
    slot = lax.bitwise_and(i, 1)

    def gather(blk, into):
        _gather_rows(lambda r: row_tok_ref[blk * tm + r], h_hbm, buf.at[into], sem.at[into], tm, ds)

    @pl.when(i == 0)
    def _():
        gather(0, 0)

    @pl.when(i < n_used)
    def _():
        _wait_rows(h_hbm, buf.at[slot], sem.at[slot], tm, ds)
        nxt = jnp.minimum(i + 1, n_used - 1)
        for r in range(tm):
            src = pl.ds(pl.multiple_of(row_tok_ref[nxt * tm + r] * ds, ds), ds)
            pltpu.make_async_copy(h_hbm.at[src], buf.at[1 - slot, pl.ds(r * pitch, ds)], sem.at[1 - slot]).start()
        for s in range(ds):
            xs_scr[:, s * LANE:(s + 1) * LANE] = _load_token_major(buf.at[slot], tm, pitch, s).astype(BF16)
        a = jnp.dot(xs_scr[...], w_in_ref[0, 0], preferred_element_type=F32)
        gate, up = a[:, :d_ff], a[:, d_ff:]
        act = (gate * jax.nn.sigmoid(gate) * up).astype(BF16)
        _store_token_major(y_ref, jnp.dot(act, w_out_ref[0, 0], preferred_element_type=F32))

        @pl.when(i + 1 >= n_used)
        def _():
            _wait_rows(h_hbm, buf.at[1 - slot], sem.at[1 - slot], tm, ds)

    @pl.when(i >= n_used)
    def _():
        y_ref[...] = jnp.zeros_like(y_ref)


def _experts(h_tm, row_tok, blk_exp, n_used, w_in, w_out, layer):
    _, e, d, f2 = w_in.shape
    ds = d // LANE
    d_ff = f2 // 2
    n_rows = row_tok.shape[0]
    tm = MOE_ROW_BLOCK
    n_blk = n_rows // tm
    kern = functools.partial(_expert_kernel, d_ff=d_ff)
    grid_spec = pltpu.PrefetchScalarGridSpec(
        num_scalar_prefetch=3,
        grid=(n_blk,),
        in_specs=[
            pl.BlockSpec(memory_space=pl.ANY),
            pl.BlockSpec((1, 1, d, f2), lambda i, rt, be, nu: (layer, be[i], 0, 0)),
            pl.BlockSpec((1, 1, d_ff, d), lambda i, rt, be, nu: (layer, be[i], 0, 0)),
        ],
        out_specs=pl.BlockSpec((tm * ds, LANE), lambda i, rt, be, nu: (i, 0)),
        scratch_shapes=[
            pltpu.VMEM((2, tm * _gather_pitch(ds), LANE), F32),
            pltpu.VMEM((tm, d), BF16),
            pltpu.SemaphoreType.DMA((2,)),
        ],
    )
    return pl.pallas_call(
        kern,
        grid_spec=grid_spec,
        out_shape=jax.ShapeDtypeStruct((n_rows * ds, LANE), F32),
        compiler_params=_params("arbitrary"),
        name="moe_experts",
    )(row_tok, blk_exp, n_used, h_tm, w_in, w_out)


def _combine_kernel(pos_ref, y_hbm, x_ref, wt_ref, gt_ref, o_ref, buf, sem):
    i = pl.program_id(0)
    tm, d = x_ref.shape
    ds = d // LANE
    slot = lax.bitwise_and(i, 1)

    def gather(tile, into):
        for k in range(TOP_K):
            _gather_rows(lambda r, k=k: pos_ref[TOP_K * (tile * tm + r) + k], y_hbm, buf.at[into, k],
                         sem.at[into, k], tm, ds)

    @pl.when(i == 0)
    def _():
        gather(0, 0)

    pitch = _gather_pitch(ds)
    for k in range(TOP_K):
        _wait_rows(y_hbm, buf.at[slot, k], sem.at[slot, k], tm, ds)
    last = pl.num_programs(0) - 1
    nxt = jnp.minimum(i + 1, last)
    for r in range(tm):
        for k in range(TOP_K):
            src = pl.ds(pl.multiple_of(pos_ref[TOP_K * (nxt * tm + r) + k] * ds, ds), ds)
            pltpu.make_async_copy(y_hbm.at[src], buf.at[1 - slot, k, pl.ds(r * pitch, ds)],
                                  sem.at[1 - slot, k]).start(priority=k)
    w0 = wt_ref[:, 0:1]
    w1 = wt_ref[:, 1:2]
    for s in range(ds):
        cols = slice(s * LANE, (s + 1) * LANE)
        moe = (_load_token_major(buf.at[slot, 0], tm, pitch, s) * w0
               + _load_token_major(buf.at[slot, 1], tm, pitch, s) * w1)
        o_ref[:, cols] = x_ref[:, cols] + gt_ref[0][:, cols] * moe

    @pl.when(i == last)
    def _():
        for k in range(TOP_K):
            _wait_rows(y_hbm, buf.at[1 - slot, k], sem.at[1 - slot, k], tm, ds)


def _combine(y_tm, pos, x2, wts, gate, seq):
    n, d = x2.shape
    ds = d // LANE
    tm = min(256, seq)
    tiles_per_batch = seq // tm
    grid_spec = pltpu.PrefetchScalarGridSpec(
        num_scalar_prefetch=1,
        grid=(n // tm,),
        in_specs=[
            pl.BlockSpec(memory_space=pl.ANY),
            pl.BlockSpec((tm, d), lambda i, p: (i, 0)),
            pl.BlockSpec((tm, TOP_K), lambda i, p: (i, 0)),
            pl.BlockSpec((1, 1, d), lambda i, p: (i // tiles_per_batch, 0, 0)),
        ],
        out_specs=pl.BlockSpec((tm, d), lambda i, p: (i, 0)),
        scratch_shapes=[
            pltpu.VMEM((2, TOP_K, tm * _gather_pitch(ds), LANE), F32),
            pltpu.SemaphoreType.DMA((2, TOP_K)),
        ],
    )
    return pl.pallas_call(
        _combine_kernel,
        grid_spec=grid_spec,
        out_shape=jax.ShapeDtypeStruct((n, d), F32),
        compiler_params=_params("arbitrary"),
        name="moe_combine",
    )(pos, y_tm, x2, wts, gate)


def _moe_layer(x2, g, shift, scale, gate, w_router_t, router_bias, w_in, w_out, layer, seq):
    n, d = x2.shape
    e = w_router_t.shape[0]
    h_tm, eid8, wt8 = _router(x2, g, shift, scale, w_router_t, router_bias, seq)
    eid = eid8[:TOP_K].T
    wts = wt8[:TOP_K].T

    rb = MOE_ROW_BLOCK
    nk = n * TOP_K
    e_flat = eid.reshape(nk)
    one_hot = (e_flat[:, None] == jnp.arange(e, dtype=jnp.int32)[None, :]).astype(jnp.int32)
    running = jnp.cumsum(one_hot, axis=0)
    counts = running[-1]
    rank = jnp.sum(running * one_hot, axis=1) - 1
    padded = ((counts + rb - 1) // rb) * rb
    ends = jnp.cumsum(padded)
    pos = ((ends - padded)[e_flat] + rank).astype(jnp.int32)
    n_rows = ((nk + e * (rb - 1) + rb - 1) // rb) * rb
    n_blk = n_rows // rb
    tok = jnp.arange(nk, dtype=jnp.int32) // TOP_K
    row_tok = jnp.zeros((n_rows,), jnp.int32).at[pos].set(tok)
    blk_start = jnp.arange(n_blk, dtype=jnp.int32) * rb
    blk_exp = jnp.sum((ends[None, :] <= blk_start[:, None]).astype(jnp.int32), axis=1)
    blk_exp = jnp.minimum(blk_exp, e - 1).astype(jnp.int32)
    n_used = (ends[-1] // rb).astype(jnp.int32).reshape(1)

    y_tm = _experts(h_tm, row_tok, blk_exp, n_used, w_in, w_out, layer)
    return _combine(y_tm, pos, x2, wts, gate, seq)


def kernel(x, c, g_attn, g_ffn, w_mod, b_mod, w_qkv_a, g_q_a, g_k_a, w_o_a, g_kv, w_mod_kv, b_mod_kv, w_kvf, b_f,
           g_k_b, w_q_b, g_q_b, w_o_b, w_router, router_bias, w_in, w_out):
    b, s, d = x.shape
    n = b * s
    depth = g_attn.shape[0]
    n_a = w_qkv_a.shape[0]
    n_heads = b_f.shape[0]
    hd = g_q_a.shape[-1]
    w = n_heads * hd
    assert hd == LANE and s % MOBA_BLOCK == 0 and d % LANE == 0

    c_pad = jnp.pad(c, ((0, 8 - b), (0, 0)))
    mod = _mod_call(c_pad, w_mod, b_mod)[:, :b]
    mod_kv = _mod_call(c_pad, w_mod_kv[None], b_mod_kv[None])[0, :b]
    slopes = jnp.exp2(-8.0 * jnp.arange(1, n_heads + 1, dtype=F32) / n_heads)
    slope_pieces = jnp.stack(_bf16_pieces(slopes * LOG2E), axis=1).reshape(3 * n_heads)
    q_scale = LOG2E * hd ** -0.5

    def logit_bound(g_q, g_k):
        bound = 1.02 * LOG2E * hd ** 0.5 * jnp.max(jnp.abs(g_q)) * jnp.max(jnp.abs(g_k))
        return bound.astype(F32).reshape(1)
    w_router_t = w_router.T
    ones_hd = jnp.ones((hd,), F32)
    w_in_bf = w_in.astype(BF16)
    w_out_bf = w_out.astype(BF16)

    x2 = x.reshape(n, d)
    kv = cum = None
    for l in range(depth):
        sh_a, sc_a, gt_a, sh_m, sc_m, gt_m = (mod[l][:, k * d:(k + 1) * d].reshape(b, 1, d) for k in range(6))
        if l < n_a:
            gains = jnp.stack([g_q_a[l] * q_scale, g_k_a[l], ones_hd]).reshape(3, 1, hd)
            qkv = _norm_proj(x2, g_attn[l], sh_a, sc_a, w_qkv_a[l].astype(BF16), gains, 2, s)
            o = _moba(qkv.reshape(b, s, 3 * w), logit_bound(g_q_a[l], g_k_a[l]), slope_pieces, n_heads)
            w_o = w_o_a[l]
        else:
            j = l - n_a
            gains = (g_q_b[j] * q_scale).reshape(1, 1, hd)
            q = _norm_proj(x2, g_attn[l], sh_a, sc_a, w_q_b[j].astype(BF16), gains, 1, s)
            o = _fox(q.reshape(b, s, w), kv, cum, logit_bound(g_q_b[j], g_k_b), n_heads)
            w_o = w_o_b[j]
        x2 = _out_proj(o.reshape(n, w), w_o.astype(BF16), x2, gt_a, s)
        x2 = _moe_layer(x2, g_ffn[l], sh_m, sc_m, gt_m, w_router_t, router_bias, w_in_bf, w_out_bf, l, s)
        if l == n_a - 1:
            sh_kv = mod_kv[:, :d].reshape(b, 1, d)
            sc_kv = mod_kv[:, d:].reshape(b, 1, d)
            gains = jnp.stack([g_k_b, ones_hd]).reshape(2, 1, hd)
            kv = _norm_proj(x2, g_kv, sh_kv, sc_kv, w_kvf[:, :2 * w].astype(BF16), gains, 1, s).reshape(b, s, 2 * w)
            w_f = jnp.pad(w_kvf[:, 2 * w:], ((0, 0), (0, LANE - n_heads)))
            b_f_pad = jnp.pad(b_f, (0, LANE - n_heads)).reshape(1, LANE)
            cum3 = _decay_cumsum(x2, g_kv, sh_kv, sc_kv, w_f, b_f_pad, b, s)
            cum = cum3[:, :, :n_heads].transpose(0, 2, 1)
    return x2.reshape(b, s, d)
```
